```python
import functools
import jax, jax.numpy as jnp
from jax import lax
import numpy as np

D_MODEL = 4096
BATCH = 4
SEQ = 2048
DEPTH = 1
DEC_BATCH = 128
DEC_SEQ = 1
PAST_LEN = 2048
PAGE_SIZE = 128

N_META = 16
N_HEADS = 16
HEAD_DIM = 128
D_ATT = N_HEADS * HEAD_DIM
D_RNN = D_MODEL // 2
N_LRU_BLOCKS = 16
LRU_BLOCK = D_RNN // N_LRU_BLOCKS
CONV_W = 4
LRU_C = 8.0
D_FF = ((8 * D_MODEL // 3 + 255) // 256) * 256
Q_BLOCK = 128
NORM_EPS = 1e-6

kernel_name = 'fox_rglru_gated_macaron_step'


def rmsnorm(x, g):
    x32 = x.astype(jnp.float32)
    y = x32 * lax.rsqrt(jnp.mean(x32 * x32, axis=-1, keepdims=True) + NORM_EPS)
    return (y * g.astype(jnp.float32)).astype(x.dtype)


def half_ffn(x, g, w1, w3, w2):
    h = rmsnorm(x, g)
    return x + 0.5 * ((jax.nn.silu(h @ w1) * (h @ w3)) @ w2)


def split_in(h, w_in, b_f, b_gate):
    B, T = h.shape[0], h.shape[1]
    cuts = np.cumsum([D_ATT, D_ATT, D_ATT, N_HEADS, D_RNN, D_RNN]).tolist()
    q, k, v, f, xb, yb, gl = jnp.split(h @ w_in, cuts, axis=-1)
    heads = lambda t: t.reshape(B, T, N_HEADS, HEAD_DIM)
    logf = jax.nn.log_sigmoid((f + b_f).astype(jnp.float32))
    g_a, g_b = jnp.split(jax.nn.sigmoid(gl + b_gate), 2, axis=-1)
    return heads(q), heads(k), heads(v), logf, xb, yb, g_a, g_b


def fox_logits(q, k, c_q, c_k, mask):
    s = jnp.einsum('bthd,bshd->bhts', q, k).astype(jnp.float32) * (HEAD_DIM ** -0.5)
    s = s + jnp.swapaxes(c_q, 1, 2)[..., :, None] - jnp.swapaxes(c_k, 1, 2)[..., None, :]
    return jnp.where(mask, s, -jnp.inf)


def fox_prompt(q, k, v, logf):
    B, L = q.shape[0], q.shape[1]
    n_blk = (L - N_META) // Q_BLOCK
    c = jnp.cumsum(logf, axis=1)
    m_mask = jnp.tril(jnp.ones((N_META, N_META), bool))
    p = jax.nn.softmax(fox_logits(q[:, :N_META], k[:, :N_META], c[:, :N_META], c[:, :N_META], m_mask), axis=-1)
    o_meta = jnp.einsum('bhts,bshd->bthd', p.astype(v.dtype), v[:, :N_META])
    key_pos = jnp.arange(L)

    def block(i):
        start = N_META + i * Q_BLOCK
        qb = lax.dynamic_slice_in_dim(q, start, Q_BLOCK, axis=1)
        cb = lax.dynamic_slice_in_dim(c, start, Q_BLOCK, axis=1)
        mask = key_pos[None, :] <= (start + jnp.arange(Q_BLOCK))[:, None]
        pb = jax.nn.softmax(fox_logits(qb, k, cb, c, mask), axis=-1)
        return jnp.einsum('bhts,bshd->bthd', pb.astype(v.dtype), v)

    o = lax.map(block, jnp.arange(n_blk))
    o = jnp.moveaxis(o, 0, 1).reshape(B, n_blk * Q_BLOCK, N_HEADS, HEAD_DIM)
    return jnp.concatenate([o_meta, o], axis=1).reshape(B, L, D_ATT)


def fox_decode(q, k, v, logf, k_past, v_past, logf_past):
    B, T = q.shape[0], q.shape[1]
    P = k_past.shape[1]
    c = jnp.cumsum(jnp.concatenate([logf_past.astype(jnp.float32), logf], axis=1), axis=1)
    c_past, c_new = c[:, :P], c[:, P:]
    s_past = fox_logits(q, k_past, c_new, c_past, jnp.ones((T, P), bool))
    s_new = fox_logits(q, k, c_new, c_new, jnp.tril(jnp.ones((T, T), bool)))
    p = jax.nn.softmax(jnp.concatenate([s_past, s_new], axis=-1), axis=-1).astype(v.dtype)
    o = jnp.einsum('bhts,bshd->bthd', p[..., :P], v_past) + jnp.einsum('bhts,bshd->bthd', p[..., P:], v)
    return o.reshape(B, T, D_ATT)


def causal_conv(xb, buf, w, b):
    T = xb.shape[1]
    xp = jnp.concatenate([buf.astype(xb.dtype), xb], axis=1)
    y = b + xp[:, 0:T] * w[0]
    for j in range(1, CONV_W):
        y = y + xp[:, j:j + T] * w[j]
    return y, xp[:, -(CONV_W - 1):]


def rg_lru(x, h0, wa, ba, wx, bx, lam):
    B, T, C = x.shape
    xr = x.reshape(B, T, N_LRU_BLOCKS, LRU_BLOCK)
    r = jax.nn.sigmoid((jnp.einsum('btnc,ncd->btnd', xr, wa).reshape(B, T, C) + ba).astype(jnp.float32))
    i = jax.nn.sigmoid((jnp.einsum('btnc,ncd->btnd', xr, wx).reshape(B, T, C) + bx).astype(jnp.float32))
    log_a = -LRU_C * r * jax.nn.softplus(-lam.astype(jnp.float32))
    a = jnp.exp(log_a)
    u = jnp.sqrt(-jnp.expm1(2.0 * log_a)) * (i * x.astype(jnp.float32))

    def step(h, au):
        h = au[0] * h + au[1]
        return h, h

    h_last, hs = lax.scan(step, h0.astype(jnp.float32), (jnp.swapaxes(a, 0, 1), jnp.swapaxes(u, 0, 1)))
    return jnp.swapaxes(hs, 0, 1).astype(x.dtype), h_last.astype(h0.dtype)


def mixer_layer(x, attn_fn, h0, conv0, norm_ffn1, ffn1_w1, ffn1_w3, ffn1_w2, norm_mix, w_in, b_f, b_gate,
                conv_w, conv_b, lru_wa, lru_ba, lru_wx, lru_bx, lru_lambda, w_pa, w_pb, w_o,
                norm_ffn2, ffn2_w1, ffn2_w3, ffn2_w2):
    x = half_ffn(x, norm_ffn1, ffn1_w1, ffn1_w3, ffn1_w2)
    h = rmsnorm(x, norm_mix)
    q, k, v, logf, xb, yb, g_a, g_b = split_in(h, w_in, b_f, b_gate)
    y_att = attn_fn(q, k, v, logf)
    xc, conv_new = causal_conv(xb, conv0, conv_w, conv_b)
    hs, h_new = rg_lru(xc, h0, lru_wa, lru_ba, lru_wx, lru_bx, lru_lambda)
    y_rnn = hs * jax.nn.gelu(yb)
    merged = g_a * (y_att @ w_pa) + g_b * (y_rnn @ w_pb)
    x = x + merged @ w_o
    x = half_ffn(x, norm_ffn2, ffn2_w1, ffn2_w3, ffn2_w2)
    return x, k, v, logf, h_new, conv_new


def setup_inputs(seed: int = 0) -> dict:
    key = jax.random.key(seed)
    ks = iter(jax.random.split(key, 48))

    def nrm(shape, scale=1.0):
        return jax.random.normal(next(ks), shape, jnp.float32) * scale

    def gain(shape):
        return 1.0 + nrm(shape, 0.05)

    n_pages = PAST_LEN // PAGE_SIZE
    n_used = DEC_BATCH * n_pages
    n_pool = n_used + max(1, n_used // 4)
    page_table = jax.random.permutation(next(ks), n_pool)[:n_used].reshape(DEC_BATCH, n_pages).astype(jnp.int32)
    a_c = jax.random.uniform(next(ks), (DEPTH, D_RNN), jnp.float32, 0.9, 0.999)
    a_base = a_c ** (1.0 / LRU_C)
    lru_lambda = jnp.log(a_base) - jnp.log1p(-a_base)
    b_f = jnp.linspace(1.0, 6.0, N_HEADS, dtype=jnp.float32)[None, :] + nrm((DEPTH, N_HEADS), 0.1)
    n_in = 3 * D_ATT + N_HEADS + 2 * D_RNN + 2 * D_MODEL
    dm = D_MODEL ** -0.5
    return {
        'x_prompt': nrm((BATCH, SEQ, D_MODEL)),
        'x_sample': nrm((DEC_BATCH, DEC_SEQ, D_MODEL)),
        'cache_k': nrm((DEPTH, n_pool, PAGE_SIZE, N_HEADS, HEAD_DIM)),
        'cache_v': nrm((DEPTH, n_pool, PAGE_SIZE, N_HEADS, HEAD_DIM)),
        'cache_logf': jax.nn.log_sigmoid(4.0 + nrm((DEPTH, n_pool, PAGE_SIZE, N_HEADS))),
        'state_h': nrm((DEPTH, DEC_BATCH, D_RNN), 0.5),
        'state_conv': nrm((DEPTH, DEC_BATCH, CONV_W - 1, D_RNN)),
        'page_table': page_table,
        'meta': nrm((N_META, D_MODEL)),
        'norm_ffn1': gain((DEPTH, D_MODEL)),
        'ffn1_w1': nrm((DEPTH, D_MODEL, D_FF), dm),
        'ffn1_w3': nrm((DEPTH, D_MODEL, D_FF), dm),
        'ffn1_w2': nrm((DEPTH, D_FF, D_MODEL), D_FF ** -0.5),
        'norm_mix': gain((DEPTH, D_MODEL)),
        'w_in': nrm((DEPTH, D_MODEL, n_in), dm),
        'b_f': b_f,
        'b_gate': nrm((DEPTH, 2 * D_MODEL), 0.01),
        'conv_w': nrm((DEPTH, CONV_W, D_RNN), CONV_W ** -0.5),
        'conv_b': nrm((DEPTH, D_RNN), 0.01),
        'lru_wa': nrm((DEPTH, N_LRU_BLOCKS, LRU_BLOCK, LRU_BLOCK), LRU_BLOCK ** -0.5),
        'lru_ba': nrm((DEPTH, D_RNN), 0.01),
        'lru_wx': nrm((DEPTH, N_LRU_BLOCKS, LRU_BLOCK, LRU_BLOCK), LRU_BLOCK ** -0.5),
        'lru_bx': nrm((DEPTH, D_RNN), 0.01),
        'lru_lambda': lru_lambda,
        'w_pa': nrm((DEPTH, D_ATT, D_MODEL), D_ATT ** -0.5),
        'w_pb': nrm((DEPTH, D_RNN, D_MODEL), D_RNN ** -0.5),
        'w_o': nrm((DEPTH, D_MODEL, D_MODEL), dm),
        'norm_ffn2': gain((DEPTH, D_MODEL)),
        'ffn2_w1': nrm((DEPTH, D_MODEL, D_FF), dm),
        'ffn2_w3': nrm((DEPTH, D_MODEL, D_FF), dm),
        'ffn2_w2': nrm((DEPTH, D_FF, D_MODEL), D_FF ** -0.5),
        'norm_final': gain((D_MODEL,)),
    }


def reference(x_prompt, x_sample, cache_k, cache_v, cache_logf, state_h, state_conv, page_table, meta,
              norm_ffn1, ffn1_w1, ffn1_w3, ffn1_w2, norm_mix, w_in, b_f, b_gate, conv_w, conv_b,
              lru_wa, lru_ba, lru_wx, lru_bx, lru_lambda, w_pa, w_pb, w_o,
              norm_ffn2, ffn2_w1, ffn2_w3, ffn2_w2, norm_final):
    B = x_prompt.shape[0]
    DB = x_sample.shape[0]
    meta_b = jnp.broadcast_to(meta.astype(x_prompt.dtype)[None], (B, N_META, meta.shape[-1]))
    xp = jnp.concatenate([meta_b, x_prompt], axis=1)
    xs = x_sample
    new_p = []
    new_s = []
    for l in range(DEPTH):
        lw = (norm_ffn1[l], ffn1_w1[l], ffn1_w3[l], ffn1_w2[l], norm_mix[l], w_in[l], b_f[l], b_gate[l],
              conv_w[l], conv_b[l], lru_wa[l], lru_ba[l], lru_wx[l], lru_bx[l], lru_lambda[l],
              w_pa[l], w_pb[l], w_o[l], norm_ffn2[l], ffn2_w1[l], ffn2_w3[l], ffn2_w2[l])
        h0 = jnp.zeros((B, D_RNN), x_prompt.dtype)
        c0 = jnp.zeros((B, CONV_W - 1, D_RNN), x_prompt.dtype)
        xp, kp_new, vp_new, lfp_new, hp_new, cp_new = mixer_layer(xp, fox_prompt, h0, c0, *lw)
        new_p.append((kp_new, vp_new, lfp_new, hp_new, cp_new))
        k_past = cache_k[l, page_table].reshape(DB, -1, N_HEADS, HEAD_DIM)
        v_past = cache_v[l, page_table].reshape(DB, -1, N_HEADS, HEAD_DIM)
        lf_past = cache_logf[l, page_table].reshape(DB, -1, N_HEADS)
        attn = functools.partial(fox_decode, k_past=k_past, v_past=v_past, logf_past=lf_past)
        xs, ks_new, vs_new, lfs_new, hs_new, cs_new = mixer_layer(xs, attn, state_h[l], state_conv[l], *lw)
        new_s.append((ks_new, vs_new, lfs_new, hs_new, cs_new))

    def stack(rows, i):
        return jnp.stack([r[i] for r in rows], axis=0)

    return (rmsnorm(xp, norm_final)[:, N_META:], rmsnorm(xs, norm_final),
            stack(new_p, 0), stack(new_p, 1), stack(new_p, 2), stack(new_p, 3), stack(new_p, 4),
            stack(new_s, 0), stack(new_s, 1), stack(new_s, 2), stack(new_s, 3), stack(new_s, 4))
```

```python
import functools

import jax
import jax.numpy as jnp
from jax import lax
from jax.experimental import pallas as pl
from jax.experimental.pallas import tpu as pltpu

F32 = jnp.float32
BF16 = jnp.bfloat16

NORM_EPS = 1e-6
LRU_C = 8.0
HEAD_DIM = 128
LRU_BLOCK = 128
META_PAD = 128
NEG_BIG = -1e30
VMEM_LIMIT_BYTES = 56 * 1024 * 1024
TM_TARGET = 1056
TN = 512
TK_TARGET = 2816
ATT_BLOCK = 256
RNN_CHUNK = 256


def _pick(n, target, mult):
    best = None
    for d in range(mult, min(n, target) + 1, mult):
        if n % d == 0:
            best = d
    assert best is not None, (n, target, mult)
    return best


def _params(*sem):
    return pltpu.CompilerParams(dimension_semantics=sem, vmem_limit_bytes=VMEM_LIMIT_BYTES)


def _dot(a, b):
    return jnp.dot(a, b, preferred_element_type=F32)


def _dot_nt(a, b):
    return lax.dot_general(a, b, (((1,), (1,)), ((), ())), preferred_element_type=F32)


def _softplus(x):
    return jnp.maximum(x, 0.0) + jnp.log1p(jnp.exp(-jnp.abs(x)))


def _rmsnorm_kernel(x_ref, g_ref, o_ref):
    x = x_ref[...]
    ms = jnp.mean(x * x, axis=-1, keepdims=True)
    o_ref[...] = ((x * lax.rsqrt(ms + NORM_EPS)) * g_ref[...]).astype(o_ref.dtype)


def rmsnorm(x, g, out_dtype, *, row_start=0, n_rows=None):
    R, D = x.shape
    n_rows = R if n_rows is None else n_rows
    tr = _pick(n_rows, 264, 16)
    assert row_start % tr == 0
    off = row_start // tr
    return pl.pallas_call(
        _rmsnorm_kernel,
        grid=(n_rows // tr,),
        in_specs=[pl.BlockSpec((tr, D), lambda i: (i + off, 0)),
                  pl.BlockSpec((1, D), lambda i: (0, 0))],
        out_specs=pl.BlockSpec((tr, D), lambda i: (i, 0)),
        out_shape=jax.ShapeDtypeStruct((n_rows, D), out_dtype),
        compiler_params=_params("parallel"),
        name="rmsnorm",
    )(x, g.reshape(1, D))


def _ffn_up_kernel(h_ref, w1_ref, w3_ref, o_ref):
    h = h_ref[...]
    a = _dot(h, w1_ref[...])
    b = _dot(h, w3_ref[...])
    o_ref[...] = (a * jax.nn.sigmoid(a) * b).astype(o_ref.dtype)


def ffn_up(h, w1, w3, tm):
    R, K = h.shape
    N = w1.shape[1]
    return pl.pallas_call(
        _ffn_up_kernel,
        grid=(R // tm, N // TN),
        in_specs=[pl.BlockSpec((tm, K), lambda i, j: (i, 0)),
                  pl.BlockSpec((K, TN), lambda i, j: (0, j)),
                  pl.BlockSpec((K, TN), lambda i, j: (0, j))],
        out_specs=pl.BlockSpec((tm, TN), lambda i, j: (i, j)),
        out_shape=jax.ShapeDtypeStruct((R, N), BF16),
        compiler_params=_params("parallel", "parallel"),
        name="ffn_up",
    )(h, w1, w3)


def _mm_res_kernel(a_ref, w_ref, res_ref, o_ref, *, scale, nk):
    part = _dot(a_ref[...], w_ref[...])
    if nk == 1:
        o_ref[...] = res_ref[...] + scale * part
        return
    k = pl.program_id(2)

    @pl.when(k == 0)
    def _():
        o_ref[...] = part

    @pl.when(jnp.logical_and(k > 0, k < nk - 1))
    def _():
        o_ref[...] += part

    @pl.when(k == nk - 1)
    def _():
        o_ref[...] = res_ref[...] + scale * (o_ref[...] + part)


def mm_residual(a, w, res, scale, tm, tn, tk):
    R, K = a.shape
    N = w.shape[1]
    nk = K // tk
    return pl.pallas_call(
        functools.partial(_mm_res_kernel, scale=scale, nk=nk),
        grid=(R // tm, N // tn, nk),
        in_specs=[pl.BlockSpec((tm, tk), lambda i, j, k: (i, k)),
                  pl.BlockSpec((tk, tn), lambda i, j, k: (k, j)),
                  pl.BlockSpec((tm, tn), lambda i, j, k: (i, j))],
        out_specs=pl.BlockSpec((tm, tn), lambda i, j, k: (i, j)),
        out_shape=jax.ShapeDtypeStruct((R, N), F32),
        compiler_params=_params("parallel", "parallel", "arbitrary"),
        name="mm_residual",
    )(a, w, res)


def _mm_dual_kernel(a_ref, w_ref, o32_ref, o16_ref):
    acc = _dot(a_ref[...], w_ref[...])
    o32_ref[...] = acc
    o16_ref[...] = acc.astype(BF16)


def mm_dual(a, w, tm):
    R, K = a.shape
    N = w.shape[1]
    return pl.pallas_call(
        _mm_dual_kernel,
        grid=(R // tm, N // TN),
        in_specs=[pl.BlockSpec((tm, K), lambda i, j: (i, 0)),
                  pl.BlockSpec((K, TN), lambda i, j: (0, j))],
        out_specs=[pl.BlockSpec((tm, TN), lambda i, j: (i, j)),
                   pl.BlockSpec((tm, TN), lambda i, j: (i, j))],
        out_shape=[jax.ShapeDtypeStruct((R, N), F32), jax.ShapeDtypeStruct((R, N), BF16)],
        compiler_params=_params("parallel", "parallel"),
        name="mm_qkv",
    )(a, w)


def _mm_gate_kernel(a_ref, w_ref, b_ref, o_ref, *, j_sig, j_lsig):
    j = pl.program_id(1)
    acc = _dot(a_ref[...], w_ref[...])

    @pl.when(j < j_sig)
    def _():
        o_ref[...] = acc

    @pl.when(jnp.logical_and(j >= j_sig, j < j_lsig))
    def _():
        o_ref[...] = jax.nn.sigmoid(acc + b_ref[...])

    @pl.when(j >= j_lsig)
    def _():
        o_ref[...] = -_softplus(-(acc + b_ref[...]))


def mm_gate(a, w, bias, tm, n_plain, n_sig):
    R, K = a.shape
    N = w.shape[1]
    return pl.pallas_call(
        functools.partial(_mm_gate_kernel, j_sig=n_plain // TN, j_lsig=(n_plain + n_sig) // TN),
        grid=(R // tm, N // TN),
        in_specs=[pl.BlockSpec((tm, K), lambda i, j: (i, 0)),
                  pl.BlockSpec((K, TN), lambda i, j: (0, j)),
                  pl.BlockSpec((1, TN), lambda i, j: (0, j))],
        out_specs=pl.BlockSpec((tm, TN), lambda i, j: (i, j)),
        out_shape=jax.ShapeDtypeStruct((R, N), F32),
        compiler_params=_params("parallel", "parallel"),
        name="mm_gate",
    )(a, w, bias)


def _merge_kernel(ya_ref, yb_ref, wa_ref, wb_ref, ga_ref, gb_ref, o_ref):
    pa = _dot(ya_ref[...], wa_ref[...])
    pb = _dot(yb_ref[...], wb_ref[...])
    o_ref[...] = (ga_ref[...] * pa + gb_ref[...] * pb).astype(o_ref.dtype)


def merge_proj(ya, yb, wa, wb, gates, ga_col, gb_col, tm):
    R, Ka = ya.shape
    Kb = yb.shape[1]
    N = wa.shape[1]
    ja, jb = ga_col // TN, gb_col // TN
    return pl.pallas_call(
        _merge_kernel,
        grid=(R // tm, N // TN),
        in_specs=[pl.BlockSpec((tm, Ka), lambda i, j: (i, 0)),
                  pl.BlockSpec((tm, Kb), lambda i, j: (i, 0)),
                  pl.BlockSpec((Ka, TN), lambda i, j: (0, j)),
                  pl.BlockSpec((Kb, TN), lambda i, j: (0, j)),
                  pl.BlockSpec((tm, TN), lambda i, j: (i, j + ja)),
                  pl.BlockSpec((tm, TN), lambda i, j: (i, j + jb))],
        out_specs=pl.BlockSpec((tm, TN), lambda i, j: (i, j)),
        out_shape=jax.ShapeDtypeStruct((R, N), BF16),
        compiler_params=_params("parallel", "parallel"),
        name="merge_proj",
    )(ya, yb, wa, wb, gates, gates)


def _split3(x):
    hi = x.astype(BF16)
    r1 = x - hi.astype(F32)
    mid = r1.astype(BF16)
    lo = (r1 - mid.astype(F32)).astype(BF16)
    return hi, mid, lo


def _lane_cumsum(x):
    n = x.shape[1]
    row = lax.broadcasted_iota(jnp.int32, (n, n), 0)
    col = lax.broadcasted_iota(jnp.int32, (n, n), 1)
    tri = jnp.where(row <= col, 1.0, 0.0).astype(BF16)
    hi, mid, lo = _split3(x)
    return _dot(hi, tri) + _dot(mid, tri) + _dot(lo, tri)


def _csum_kernel(lf_ref, o_ref, carry_s, *, n_heads, n_valid, subtract_total):
    blk = pl.program_id(1)

    @pl.when(blk == 0)
    def _():
        carry_s[...] = jnp.zeros_like(carry_s)

    lf = lf_ref[...]
    tb = lf.shape[0]
    if n_valid < tb:
        row = lax.broadcasted_iota(jnp.int32, lf.shape, 0)
        lf = jnp.where(row < n_valid, lf, 0.0)
    lt = lf.T[:n_heads]
    c = _lane_cumsum(lt) + carry_s[:, 0:1]
    last = c[:, tb - 1:tb]
    carry_s[...] = jnp.broadcast_to(last, carry_s.shape)
    if subtract_total:
        c = c - last
    o_ref[0] = c


def log_forget_cumsum(gates, f_col, row_start, n_seq, seq_len, n_heads, *, tb, n_valid=None,
                      subtract_total=False):
    n_valid = tb if n_valid is None else n_valid
    nb = seq_len // tb
    assert row_start % tb == 0 and f_col % 128 == 0
    r0, c0 = row_start // tb, f_col // 128
    return pl.pallas_call(
        functools.partial(_csum_kernel, n_heads=n_heads, n_valid=n_valid, subtract_total=subtract_total),
        grid=(n_seq, nb),
        in_specs=[pl.BlockSpec((tb, 128), lambda s, b: (r0 + s * nb + b, c0))],
        out_specs=pl.BlockSpec((1, n_heads, tb), lambda s, b: (s, 0, b)),
        out_shape=jax.ShapeDtypeStruct((n_seq, n_heads, seq_len), F32),
        scratch_shapes=[pltpu.VMEM((n_heads, 128), F32)],
        compiler_params=_params("parallel", "arbitrary"),
        name="logf_cumsum",
    )(gates)


def _online_update(carry, s, v):
    m, l, acc = carry
    m_new = jnp.maximum(m, jnp.max(s, axis=1, keepdims=True))
    alpha = jnp.exp(m - m_new)
    p = jnp.exp(s - m_new)
    l_new = alpha * l + jnp.sum(p, axis=1, keepdims=True)
    acc_new = alpha * acc + _dot(p.astype(BF16), v)
    return m_new, l_new, acc_new


def _pattn_kernel(q_ref, k_ref, v_ref, ck_ref, km_ref, vm_ref, cm_ref, o_ref, *, n_meta, blk):
    T = q_ref.shape[0]
    scale = HEAD_DIM ** -0.5
    km = km_ref[...]
    vm = vm_ref[...]
    lane = lax.broadcasted_iota(jnp.int32, (1, km.shape[0]), 1)
    bias_m = jnp.where(lane < n_meta, -cm_ref[0, 0], NEG_BIG)
    row = lax.broadcasted_iota(jnp.int32, (blk, blk), 0)
    col = lax.broadcasted_iota(jnp.int32, (blk, blk), 1)
    causal = col <= row

    def q_block(qi, _):
        q0 = pl.multiple_of(qi * blk, blk)
        q = q_ref[pl.ds(q0, blk), :]
        sm = _dot_nt(q, km) * scale + bias_m
        m0 = jnp.max(sm, axis=1, keepdims=True)
        p0 = jnp.exp(sm - m0)
        carry = (m0, jnp.sum(p0, axis=1, keepdims=True), _dot(p0.astype(BF16), vm))

        def kv_block(ki, carry):
            k0 = pl.multiple_of(ki * blk, blk)
            s = _dot_nt(q, k_ref[pl.ds(k0, blk), :]) * scale - ck_ref[0, 0, pl.ds(ki, 1), :]
            return _online_update(carry, s, v_ref[pl.ds(k0, blk), :])

        carry = lax.fori_loop(0, qi, kv_block, carry)
        s = _dot_nt(q, k_ref[pl.ds(q0, blk), :]) * scale - ck_ref[0, 0, pl.ds(qi, 1), :]
        s = jnp.where(causal, s, NEG_BIG)
        _, l, acc = _online_update(carry, s, v_ref[pl.ds(q0, blk), :])
        o_ref[pl.ds(q0, blk), :] = (acc / l).astype(o_ref.dtype)
        return 0

    lax.fori_loop(0, T // blk, q_block, 0)


def prompt_attention(qkv16, c_prompt, c_meta, B, T, H, meta_row, n_meta):
    blk = _pick(T, ATT_BLOCK, 128)
    ck = c_prompt.reshape(B, H, T // blk, blk)
    cm = c_meta.reshape(1, H, 1, META_PAD)
    mrow = meta_row // META_PAD
    return pl.pallas_call(
        functools.partial(_pattn_kernel, n_meta=n_meta, blk=blk),
        grid=(B, H),
        in_specs=[pl.BlockSpec((T, HEAD_DIM), lambda b, h: (b, h)),
                  pl.BlockSpec((T, HEAD_DIM), lambda b, h: (b, H + h)),
                  pl.BlockSpec((T, HEAD_DIM), lambda b, h: (b, 2 * H + h)),
                  pl.BlockSpec((1, 1, T // blk, blk), lambda b, h: (b, h, 0, 0)),
                  pl.BlockSpec((META_PAD, HEAD_DIM), lambda b, h: (mrow, H + h)),
                  pl.BlockSpec((META_PAD, HEAD_DIM), lambda b, h: (mrow, 2 * H + h)),
                  pl.BlockSpec((1, 1, 1, META_PAD), lambda b, h: (0, h, 0, 0))],
        out_specs=pl.BlockSpec((T, HEAD_DIM), lambda b, h: (b, h)),
        out_shape=jax.ShapeDtypeStruct((B * T, H * HEAD_DIM), BF16),
        compiler_params=_params("parallel", "parallel"),
        name="prompt_attention",
    )(qkv16, qkv16, qkv16, ck, qkv16, qkv16, cm)


def _mattn_kernel(q_ref, k_ref, v_ref, cm_ref, o_ref, *, n_meta):
    scale = HEAD_DIM ** -0.5
    n = q_ref.shape[0]
    row = lax.broadcasted_iota(jnp.int32, (n, n), 0)
    col = lax.broadcasted_iota(jnp.int32, (n, n), 1)
    visible = jnp.logical_and(col < n_meta, jnp.logical_or(col <= row, row >= n_meta))
    s = _dot_nt(q_ref[...], k_ref[...]) * scale - cm_ref[0, 0]
    s = jnp.where(visible, s, NEG_BIG)
    p = jnp.exp(s - jnp.max(s, axis=1, keepdims=True))
    l = jnp.sum(p, axis=1, keepdims=True)
    o_ref[...] = (_dot(p.astype(BF16), v_ref[...]) / l).astype(o_ref.dtype)


def meta_attention(qkv16, c_meta, H, meta_row, n_meta):
    mrow = meta_row // META_PAD
    cm = c_meta.reshape(1, H, 1, META_PAD)
    return pl.pallas_call(
        functools.partial(_mattn_kernel, n_meta=n_meta),
        grid=(H,),
        in_specs=[pl.BlockSpec((META_PAD, HEAD_DIM), lambda h: (mrow, h)),
                  pl.BlockSpec((META_PAD, HEAD_DIM), lambda h: (mrow, H + h)),
                  pl.BlockSpec((META_PAD, HEAD_DIM), lambda h: (mrow, 2 * H + h)),
                  pl.BlockSpec((1, 1, 1, META_PAD), lambda h: (0, h, 0, 0))],
        out_specs=pl.BlockSpec((META_PAD, HEAD_DIM), lambda h: (0, h)),
        out_shape=jax.ShapeDtypeStruct((META_PAD, H * HEAD_DIM), BF16),
        compiler_params=_params("parallel"),
        name="meta_attention",
    )(qkv16, qkv16, qkv16, cm)


def _dattn_kernel(pt_ref, q_ref, kn_ref, vn_ref, lfn_ref, kp_ref, vp_ref, lfp_ref, o_ref,
                  m_s, l_s, acc_s, c_s, *, n_pages):
    del pt_ref
    p_id = pl.program_id(1)
    H = q_ref.shape[1]
    scale = HEAD_DIM ** -0.5

    @pl.when(p_id == 0)
    def _():
        m_s[...] = jnp.full_like(m_s, NEG_BIG)
        l_s[...] = jnp.zeros_like(l_s)
        acc_s[...] = jnp.zeros_like(acc_s)
        c_s[...] = jnp.zeros_like(c_s)

    q = q_ref[0]
    head_row = lax.broadcasted_iota(jnp.int32, (H, H * HEAD_DIM), 0)
    head_col = lax.broadcasted_iota(jnp.int32, (H, H * HEAD_DIM), 1) // HEAD_DIM
    q_wide = jnp.concatenate([q] * H, axis=1)
    q_bd = jnp.where(head_row == head_col, q_wide, jnp.zeros_like(q_wide))
    c = _lane_cumsum(lfp_ref[0]) + c_s[:, 0:1]
    s = _dot_nt(q_bd, kp_ref[0].astype(BF16)) * scale - c
    m, l, acc = _online_update((m_s[:, 0:1], l_s[:, 0:1], acc_s[...]), s, vp_ref[0].astype(BF16))
    m_s[...] = jnp.broadcast_to(m, m_s.shape)
    l_s[...] = jnp.broadcast_to(l, l_s.shape)
    acc_s[...] = acc
    c_last = c[:, c.shape[1] - 1:]
    c_s[...] = jnp.broadcast_to(c_last, c_s.shape)

    @pl.when(p_id == n_pages - 1)
    def _():
        c_new = c_last + lfn_ref[0]
        s_new = jnp.sum(q.astype(F32) * kn_ref[0].astype(F32), axis=1, keepdims=True) * scale - c_new
        m2 = jnp.maximum(m, s_new)
        alpha = jnp.exp(m - m2)
        p_new = jnp.exp(s_new - m2)
        l2 = alpha * l + p_new
        diag = jnp.zeros((H, HEAD_DIM), F32)
        rows = lax.broadcasted_iota(jnp.int32, (H, HEAD_DIM), 0)
        for j in range(H):
            diag = diag + jnp.where(rows == j, acc[:, j * HEAD_DIM:(j + 1) * HEAD_DIM], 0.0)
        o_ref[0] = ((alpha * diag + p_new * vn_ref[0].astype(F32)) / l2).astype(o_ref.dtype)


def decode_attention(page_table, q, k_new, v_new, lf_new, cache_k, cache_v, cache_lf_t):
    DB, H, _ = q.shape
    n_pages = page_table.shape[1]
    page = cache_k.shape[1]
    grid_spec = pltpu.PrefetchScalarGridSpec(
        num_scalar_prefetch=1,
        grid=(DB, n_pages),
        in_specs=[pl.BlockSpec((1, H, HEAD_DIM), lambda b, p, pt: (b, 0, 0)),
                  pl.BlockSpec((1, H, HEAD_DIM), lambda b, p, pt: (b, 0, 0)),
                  pl.BlockSpec((1, H, HEAD_DIM), lambda b, p, pt: (b, 0, 0)),
                  pl.BlockSpec((1, H, 1), lambda b, p, pt: (b, 0, 0)),
                  pl.BlockSpec((1, page, H * HEAD_DIM), lambda b, p, pt: (pt[b * n_pages + p], 0, 0)),
                  pl.BlockSpec((1, page, H * HEAD_DIM), lambda b, p, pt: (pt[b * n_pages + p], 0, 0)),
                  pl.BlockSpec((1, H, page), lambda b, p, pt: (pt[b * n_pages + p], 0, 0))],
        out_specs=pl.BlockSpec((1, H, HEAD_DIM), lambda b, p, pt: (b, 0, 0)),
        scratch_shapes=[pltpu.VMEM((H, 128), F32), pltpu.VMEM((H, 128), F32),
                        pltpu.VMEM((H, H * HEAD_DIM), F32), pltpu.VMEM((H, 128), F32)],
    )
    return pl.pallas_call(
        functools.partial(_dattn_kernel, n_pages=n_pages),
        grid_spec=grid_spec,
        out_shape=jax.ShapeDtypeStruct((DB, H, HEAD_DIM), BF16),
        compiler_params=_params("parallel", "arbitrary"),
        name="decode_attention",
    )(page_table.reshape(-1), q, k_new, v_new, lf_new, cache_k, cache_v, cache_lf_t)


def _lru_gates(xc, wa_ref, ba, wx_ref, bx, lam):
    xcb = xc.astype(BF16)
    nb = xc.shape[1] // LRU_BLOCK
    ra = jnp.concatenate([_dot(xcb[:, n * LRU_BLOCK:(n + 1) * LRU_BLOCK], wa_ref[n]) for n in range(nb)], axis=1)
    rx = jnp.concatenate([_dot(xcb[:, n * LRU_BLOCK:(n + 1) * LRU_BLOCK], wx_ref[n]) for n in range(nb)], axis=1)
    r = jax.nn.sigmoid(ra + ba)
    i = jax.nn.sigmoid(rx + bx)
    log_a = (-LRU_C) * r * _softplus(-lam)
    a = jnp.exp(log_a)
    u = jnp.sqrt(-jnp.tanh(log_a) * (a * a + 1.0)) * (i * xc)
    return a, u


def _rnn_kernel(xb_ref, yb_ref, h0_ref, c0_ref, cw_ref, cb_ref, wa_ref, ba_ref, wx_ref, bx_ref, lam_ref,
                y_ref, hl_ref, xprev_s, h_s, a_s, u_s):
    @pl.when(pl.program_id(1) == 0)
    def _():
        xprev_s[...] = c0_ref[0]
        h_s[...] = h0_ref[0]

    xb = xb_ref[...]
    n = xb.shape[0]
    xp = jnp.concatenate([xprev_s[...], xb], axis=0)
    xc = cb_ref[...] + pltpu.roll(xp, 3, 0)[8:] * cw_ref[0:1, :]
    xc = xc + pltpu.roll(xp, 2, 0)[8:] * cw_ref[1:2, :]
    xc = xc + pltpu.roll(xp, 1, 0)[8:] * cw_ref[2:3, :]
    xc = xc + xb * cw_ref[3:4, :]
    a, u = _lru_gates(xc, wa_ref, ba_ref[...], wx_ref, bx_ref[...], lam_ref[...])
    a_s[...] = a
    u_s[...] = u

    def step(t, h):
        h = a_s[pl.ds(t, 1), :] * h + u_s[pl.ds(t, 1), :]
        u_s[pl.ds(t, 1), :] = h
        return h

    h = lax.fori_loop(0, n, step, h_s[0:1, :], unroll=8)
    h_s[...] = jnp.broadcast_to(h, h_s.shape)
    hl_ref[0] = h_s[...]
    xprev_s[...] = xb[n - 8:, :]
    y_ref[...] = (u_s[...] * jax.nn.gelu(yb_ref[...])).astype(y_ref.dtype)


def rnn_sequences(gates, xb_col, yb_col, row_start, n_seq, seq_len, chunk, h0, c0, lru):
    cw, cb, wa, ba, wx, bx, lam = lru
    C = cw.shape[1]
    nb = C // LRU_BLOCK
    nc = seq_len // chunk
    assert row_start % chunk == 0 and xb_col % C == 0 and yb_col % C == 0
    r0, jx, jy = row_start // chunk, xb_col // C, yb_col // C
    vec = pl.BlockSpec((1, C), lambda s, c: (0, 0))
    return pl.pallas_call(
        _rnn_kernel,
        grid=(n_seq, nc),
        in_specs=[pl.BlockSpec((chunk, C), lambda s, c: (r0 + s * nc + c, jx)),
                  pl.BlockSpec((chunk, C), lambda s, c: (r0 + s * nc + c, jy)),
                  pl.BlockSpec((1, 8, C), lambda s, c: (s, 0, 0)),
                  pl.BlockSpec((1, 8, C), lambda s, c: (s, 0, 0)),
                  pl.BlockSpec((cw.shape[0], C), lambda s, c: (0, 0)),
                  vec,
                  pl.BlockSpec((nb, LRU_BLOCK, LRU_BLOCK), lambda s, c: (0, 0, 0)),
                  vec,
                  pl.BlockSpec((nb, LRU_BLOCK, LRU_BLOCK), lambda s, c: (0, 0, 0)),
                  vec, vec],
        out_specs=[pl.BlockSpec((chunk, C), lambda s, c: (s * nc + c, 0)),
                   pl.BlockSpec((1, 8, C), lambda s, c: (s, 0, 0))],
        out_shape=[jax.ShapeDtypeStruct((n_seq * seq_len, C), BF16),
                   jax.ShapeDtypeStruct((n_seq, 8, C), F32)],
        scratch_shapes=[pltpu.VMEM((8, C), F32), pltpu.VMEM((8, C), F32),
                        pltpu.VMEM((chunk, C), F32), pltpu.VMEM((chunk, C), F32)],
        compiler_params=_params("parallel", "arbitrary"),
        name="rnn_sequences",
    )(gates, gates, h0, c0, cw, cb, wa, ba, wx, bx, lam)


def _rnn_step_kernel(xb_ref, yb_ref, h0_ref, s0_ref, s1_ref, s2_ref, cw_ref, cb_ref, wa_ref, ba_ref,
                     wx_ref, bx_ref, lam_ref, y_ref, h_ref):
    xb = xb_ref[...]
    xc = cb_ref[...] + s0_ref[...] * cw_ref[0:1, :]
    xc = xc + s1_ref[...] * cw_ref[1:2, :]
    xc = xc + s2_ref[...] * cw_ref[2:3, :]
    xc = xc + xb * cw_ref[3:4, :]
    a, u = _lru_gates(xc, wa_ref, ba_ref[...], wx_ref, bx_ref[...], lam_ref[...])
    h = a * h0_ref[...] + u
    h_ref[...] = h
    y_ref[...] = (h * jax.nn.gelu(yb_ref[...])).astype(y_ref.dtype)


def rnn_step(gates, xb_col, yb_col, row_start, n_rows, h0, s0, s1, s2, lru):
    cw, cb, wa, ba, wx, bx, lam = lru
    C = cw.shape[1]
    nb = C // LRU_BLOCK
    assert row_start % n_rows == 0
    r0, jx, jy = row_start // n_rows, xb_col // C, yb_col // C
    full = pl.BlockSpec((n_rows, C), lambda i: (0, 0))
    vec = pl.BlockSpec((1, C), lambda i: (0, 0))
    return pl.pallas_call(
        _rnn_step_kernel,
        grid=(1,),
        in_specs=[pl.BlockSpec((n_rows, C), lambda i: (r0, jx)),
                  pl.BlockSpec((n_rows, C), lambda i: (r0, jy)),
                  full, full, full, full,
                  pl.BlockSpec((cw.shape[0], C), lambda i: (0, 0)),
                  vec,
                  pl.BlockSpec((nb, LRU_BLOCK, LRU_BLOCK), lambda i: (0, 0, 0)),
                  vec,
                  pl.BlockSpec((nb, LRU_BLOCK, LRU_BLOCK), lambda i: (0, 0, 0)),
                  vec, vec],
        out_specs=[full, full],
        out_shape=[jax.ShapeDtypeStruct((n_rows, C), BF16), jax.ShapeDtypeStruct((n_rows, C), F32)],
        compiler_params=_params("arbitrary"),
        name="rnn_step",
    )(gates, gates, h0, s0, s1, s2, cw, cb, wa, ba, wx, bx, lam)


def _half_ffn(x, g, w1, w3, w2, tm):
    D, FF = w1.shape
    ffp = -(-FF // 1024) * 1024
    w1p = jnp.pad(w1.astype(BF16), ((0, 0), (0, ffp - FF)))
    w3p = jnp.pad(w3.astype(BF16), ((0, 0), (0, ffp - FF)))
    w2p = jnp.pad(w2.astype(BF16), ((0, ffp - FF), (0, 0)))
    h = rmsnorm(x, g, BF16)
    act = ffn_up(h, w1p, w3p, tm)
    return mm_residual(act, w2p, x, 0.5, tm, _pick(D, 1024, 128), _pick(ffp, TK_TARGET, 256))


def kernel(x_prompt, x_sample, cache_k, cache_v, cache_logf, state_h, state_conv, page_table, meta,
           norm_ffn1, ffn1_w1, ffn1_w3, ffn1_w2, norm_mix, w_in, b_f, b_gate, conv_w, conv_b,
           lru_wa, lru_ba, lru_wx, lru_bx, lru_lambda, w_pa, w_pb, w_o,
           norm_ffn2, ffn2_w1, ffn2_w3, ffn2_w2, norm_final):
    B, T, D = x_prompt.shape
    DB, dec_seq, _ = x_sample.shape
    depth, n_pool, page, H, hd = cache_k.shape
    n_meta = meta.shape[0]
    C = state_h.shape[-1]
    conv_width = conv_w.shape[1]
    DA = H * hd
    assert depth == 1 and dec_seq == 1 and hd == HEAD_DIM and conv_width == 4
    assert n_meta <= META_PAD and DB % 128 == 0 and page == 128
    assert w_in.shape[-1] == 3 * DA + H + 2 * C + 2 * D

    Rm = B * T
    s_row, m_row = Rm, Rm + DB
    R = Rm + DB + META_PAD
    tm = _pick(R, TM_TARGET, 16)

    x0 = jnp.concatenate([x_prompt.reshape(Rm, D), x_sample.reshape(DB, D), meta.astype(F32),
                          jnp.zeros((META_PAD - n_meta, D), F32)], axis=0)

    x1 = _half_ffn(x0, norm_ffn1[0], ffn1_w1[0], ffn1_w3[0], ffn1_w2[0], tm)

    h2 = rmsnorm(x1, norm_mix[0], BF16)
    w = w_in[0]
    o_f = 3 * DA
    o_x = o_f + H
    w_qkv = w[:, :o_f].astype(BF16)
    f_pad = TN - H
    w_rest = jnp.concatenate([w[:, o_x:], w[:, o_f:o_x], jnp.zeros((D, f_pad), F32)], axis=1).astype(BF16)
    bias = jnp.concatenate([jnp.zeros((2 * C,), F32), b_gate[0], b_f[0], jnp.zeros((f_pad,), F32)]).reshape(1, -1)
    qkv32, qkv16 = mm_dual(h2, w_qkv, tm)
    gates = mm_gate(h2, w_rest, bias, tm, 2 * C, 2 * D)
    col_ga, col_gb, col_f = 2 * C, 2 * C + D, 2 * C + 2 * D

    c_prompt = log_forget_cumsum(gates, col_f, 0, B, T, H, tb=_pick(T, 256, 128))
    c_meta = log_forget_cumsum(gates, col_f, m_row, 1, META_PAD, H, tb=META_PAD, n_valid=n_meta,
                               subtract_total=True)
    y_att_p = prompt_attention(qkv16, c_prompt, c_meta, B, T, H, m_row, n_meta)
    y_att_m = meta_attention(qkv16, c_meta, H, m_row, n_meta)
    samp16 = qkv16[s_row:s_row + DB]
    heads = lambda t: t.reshape(DB, H, HEAD_DIM)
    lf_s = gates[s_row:s_row + DB, col_f:col_f + H]
    y_att_s = decode_attention(
        page_table, heads(samp16[:, :DA]), heads(samp16[:, DA:2 * DA]), heads(samp16[:, 2 * DA:]),
        lf_s.reshape(DB, H, 1), cache_k[0].reshape(n_pool, page, DA), cache_v[0].reshape(n_pool, page, DA),
        jnp.swapaxes(cache_logf[0], 1, 2))
    y_att = jnp.concatenate([y_att_p, y_att_s.reshape(DB, DA), y_att_m], axis=0)

    lru = (conv_w[0], conv_b[0].reshape(1, C), lru_wa[0].astype(BF16), lru_ba[0].reshape(1, C),
           lru_wx[0].astype(BF16), lru_bx[0].reshape(1, C), lru_lambda[0].reshape(1, C))
    n_meta8 = -(-n_meta // 8) * 8
    assert n_meta8 == n_meta and n_meta >= 8
    zeros8 = jnp.zeros((1, 8, C), F32)
    y_rnn_m, h_meta = rnn_sequences(gates, 0, C, m_row, 1, n_meta, n_meta, zeros8, zeros8, lru)
    xb_meta_tail = gates[m_row + n_meta - 8:m_row + n_meta, :C]
    y_rnn_p, h_prompt = rnn_sequences(
        gates, 0, C, 0, B, T, _pick(T, RNN_CHUNK, 8),
        jnp.broadcast_to(h_meta, (B, 8, C)), jnp.broadcast_to(xb_meta_tail[None], (B, 8, C)), lru)
    sc = state_conv[0]
    y_rnn_s, h_sample = rnn_step(gates, 0, C, s_row, DB, state_h[0], sc[:, 0], sc[:, 1], sc[:, 2], lru)
    y_rnn = jnp.concatenate([y_rnn_p, y_rnn_s, y_rnn_m, jnp.zeros((META_PAD - n_meta, C), BF16)], axis=0)

    merged = merge_proj(y_att, y_rnn, w_pa[0].astype(BF16), w_pb[0].astype(BF16), gates, col_ga, col_gb, tm)
    x2 = mm_residual(merged, w_o[0].astype(BF16), x1, 1.0, tm, TN, D)
    x3 = _half_ffn(x2, norm_ffn2[0], ffn2_w1[0], ffn2_w3[0], ffn2_w2[0], tm)

    y_prompt = rmsnorm(x3, norm_final, F32, row_start=0, n_rows=Rm).reshape(B, T, D)
    y_sample = rmsnorm(x3, norm_final, F32, row_start=s_row, n_rows=DB).reshape(DB, 1, D)

    def with_meta(full, lo, hi, tail):
        real = full[:Rm, lo:hi].reshape((B, T) + tail)
        m = jnp.broadcast_to(full[m_row:m_row + n_meta, lo:hi].reshape((1, n_meta) + tail), (B, n_meta) + tail)
        return jnp.concatenate([m, real], axis=1)[None]

    new_k_p = with_meta(qkv32, DA, 2 * DA, (H, HEAD_DIM))
    new_v_p = with_meta(qkv32, 2 * DA, 3 * DA, (H, HEAD_DIM))
    new_lf_p = with_meta(gates, col_f, col_f + H, (H,))
    new_h_p = h_prompt[:, 0][None]
    xb_p = gates[:Rm, :C].reshape(B, T, C)
    new_conv_p = xb_p[:, T - (conv_width - 1):][None]
    samp32 = qkv32[s_row:s_row + DB]
    new_k_s = samp32[:, DA:2 * DA].reshape(1, DB, 1, H, HEAD_DIM)
    new_v_s = samp32[:, 2 * DA:].reshape(1, DB, 1, H, HEAD_DIM)
    new_lf_s = lf_s.reshape(1, DB, 1, H)
    new_h_s = h_sample[None]
    new_conv_s = jnp.concatenate([sc[:, 1:], gates[s_row:s_row + DB, :C][:, None]], axis=1)[None]
    return (y_prompt, y_sample, new_k_p, new_v_p, new_lf_p, new_h_p, new_conv_p,
            new_k_s, new_v_s, new_lf_s, new_h_s, new_conv_s)
```

```python
import functools

import jax
import jax.numpy as jnp
from jax import lax
from jax.experimental import pallas as pl
from jax.experimental.pallas import tpu as pltpu

F32 = jnp.float32
BF16 = jnp.bfloat16

NORM_EPS = 1e-6
LRU_C = 8.0
HEAD_DIM = 128
LRU_BLOCK = 128
META_PAD = 128
NEG_BIG = -1e30
VMEM_LIMIT_BYTES = 56 * 1024 * 1024
TM_TARGET = 1056
TN = 512
TK_TARGET = 2816
ATT_BLOCK = 256
ATT_HEADS = 4
DEC_PAGES = 4
RNN_CHUNK = 256


def _pick(n, target, mult):
    best = None
    for d in range(mult, min(n, target) + 1, mult):
        if n % d == 0:
            best = d
    assert best is not None, (n, target, mult)
    return best


def _params(*sem):
    return pltpu.CompilerParams(dimension_semantics=sem, vmem_limit_bytes=VMEM_LIMIT_BYTES)


def _dot(a, b):
    return jnp.dot(a, b, preferred_element_type=F32)


def _dot_nt(a, b):
    return lax.dot_general(a, b, (((1,), (1,)), ((), ())), preferred_element_type=F32)


def _softplus(x):
    return jnp.maximum(x, 0.0) + jnp.log1p(jnp.exp(-jnp.abs(x)))


def _rmsnorm_kernel(x_ref, g_ref, o_ref):
    x = x_ref[...]
    ms = jnp.mean(x * x, axis=-1, keepdims=True)
    o_ref[...] = ((x * lax.rsqrt(ms + NORM_EPS)) * g_ref[...]).astype(o_ref.dtype)


def rmsnorm(x, g, out_dtype, *, row_start=0, n_rows=None):
    R, D = x.shape
    n_rows = R if n_rows is None else n_rows
    tr = _pick(n_rows, 264, 16)
    assert row_start % tr == 0
    off = row_start // tr
    return pl.pallas_call(
        _rmsnorm_kernel,
        grid=(n_rows // tr,),
        in_specs=[pl.BlockSpec((tr, D), lambda i: (i + off, 0)),
                  pl.BlockSpec((1, D), lambda i: (0, 0))],
        out_specs=pl.BlockSpec((tr, D), lambda i: (i, 0)),
        out_shape=jax.ShapeDtypeStruct((n_rows, D), out_dtype),
        compiler_params=_params("parallel"),
        name="rmsnorm",
    )(x, g.reshape(1, D))


def _ffn_up_kernel(h_ref, w1_ref, w3_ref, o_ref):
    h = h_ref[...]
    a = _dot(h, w1_ref[...])
    b = _dot(h, w3_ref[...])
    o_ref[...] = (a * jax.nn.sigmoid(a) * b).astype(o_ref.dtype)


def ffn_up(h, w1, w3, tm):
    R, K = h.shape
    N = w1.shape[1]
    return pl.pallas_call(
        _ffn_up_kernel,
        grid=(R // tm, N // TN),
        in_specs=[pl.BlockSpec((tm, K), lambda i, j: (i, 0)),
                  pl.BlockSpec((K, TN), lambda i, j: (0, j)),
                  pl.BlockSpec((K, TN), lambda i, j: (0, j))],
        out_specs=pl.BlockSpec((tm, TN), lambda i, j: (i, j)),
        out_shape=jax.ShapeDtypeStruct((R, N), BF16),
        compiler_params=_params("parallel", "parallel"),
        name="ffn_up",
    )(h, w1, w3)


def _mm_res_kernel(a_ref, w_ref, res_ref, o_ref, *, scale, nk):
    part = _dot(a_ref[...], w_ref[...])
    if nk == 1:
        o_ref[...] = res_ref[...] + scale * part
        return
    k = pl.program_id(2)

    @pl.when(k == 0)
    def _():
        o_ref[...] = part

    @pl.when(jnp.logical_and(k > 0, k < nk - 1))
    def _():
        o_ref[...] += part

    @pl.when(k == nk - 1)
    def _():
        o_ref[...] = res_ref[...] + scale * (o_ref[...] + part)


def mm_residual(a, w, res, scale, tm, tn, tk):
    R, K = a.shape
    N = w.shape[1]
    nk = K // tk
    return pl.pallas_call(
        functools.partial(_mm_res_kernel, scale=scale, nk=nk),
        grid=(R // tm, N // tn, nk),
        in_specs=[pl.BlockSpec((tm, tk), lambda i, j, k: (i, k)),
                  pl.BlockSpec((tk, tn), lambda i, j, k: (k, j)),
                  pl.BlockSpec((tm, tn), lambda i, j, k: (i, j))],
        out_specs=pl.BlockSpec((tm, tn), lambda i, j, k: (i, j)),
        out_shape=jax.ShapeDtypeStruct((R, N), F32),
        compiler_params=_params("parallel", "parallel", "arbitrary"),
        name="mm_residual",
    )(a, w, res)


def _mm_dual_kernel(a_ref, w_ref, o32_ref, o16_ref):
    acc = _dot(a_ref[...], w_ref[...])
    o32_ref[...] = acc
    o16_ref[...] = acc.astype(BF16)


def mm_dual(a, w, tm):
    R, K = a.shape
    N = w.shape[1]
    return pl.pallas_call(
        _mm_dual_kernel,
        grid=(R // tm, N // TN),
        in_specs=[pl.BlockSpec((tm, K), lambda i, j: (i, 0)),
                  pl.BlockSpec((K, TN), lambda i, j: (0, j))],
        out_specs=[pl.BlockSpec((tm, TN), lambda i, j: (i, j)),
                   pl.BlockSpec((tm, TN), lambda i, j: (i, j))],
        out_shape=[jax.ShapeDtypeStruct((R, N), F32), jax.ShapeDtypeStruct((R, N), BF16)],
        compiler_params=_params("parallel", "parallel"),
        name="mm_qkv",
    )(a, w)


def _mm_gate_kernel(a_ref, w_ref, b_ref, o_ref, *, j_sig, j_lsig):
    j = pl.program_id(1)
    acc = _dot(a_ref[...], w_ref[...])

    @pl.when(j < j_sig)
    def _():
        o_ref[...] = acc

    @pl.when(jnp.logical_and(j >= j_sig, j < j_lsig))
    def _():
        o_ref[...] = jax.nn.sigmoid(acc + b_ref[...])

    @pl.when(j >= j_lsig)
    def _():
        o_ref[...] = -_softplus(-(acc + b_ref[...]))


def mm_gate(a, w, bias, tm, n_plain, n_sig):
    R, K = a.shape
    N = w.shape[1]
    return pl.pallas_call(
        functools.partial(_mm_gate_kernel, j_sig=n_plain // TN, j_lsig=(n_plain + n_sig) // TN),
        grid=(R // tm, N // TN),
        in_specs=[pl.BlockSpec((tm, K), lambda i, j: (i, 0)),
                  pl.BlockSpec((K, TN), lambda i, j: (0, j)),
                  pl.BlockSpec((1, TN), lambda i, j: (0, j))],
        out_specs=pl.BlockSpec((tm, TN), lambda i, j: (i, j)),
        out_shape=jax.ShapeDtypeStruct((R, N), F32),
        compiler_params=_params("parallel", "parallel"),
        name="mm_gate",
    )(a, w, bias)


def _merge_kernel(ya_ref, yb_ref, wa_ref, wb_ref, ga_ref, gb_ref, o_ref):
    pa = _dot(ya_ref[...], wa_ref[...])
    pb = _dot(yb_ref[...], wb_ref[...])
    o_ref[...] = (ga_ref[...] * pa + gb_ref[...] * pb).astype(o_ref.dtype)


def merge_proj(ya, yb, wa, wb, gates, ga_col, gb_col, tm):
    R, Ka = ya.shape
    Kb = yb.shape[1]
    N = wa.shape[1]
    ja, jb = ga_col // TN, gb_col // TN
    return pl.pallas_call(
        _merge_kernel,
        grid=(R // tm, N // TN),
        in_specs=[pl.BlockSpec((tm, Ka), lambda i, j: (i, 0)),
                  pl.BlockSpec((tm, Kb), lambda i, j: (i, 0)),
                  pl.BlockSpec((Ka, TN), lambda i, j: (0, j)),
                  pl.BlockSpec((Kb, TN), lambda i, j: (0, j)),
                  pl.BlockSpec((tm, TN), lambda i, j: (i, j + ja)),
                  pl.BlockSpec((tm, TN), lambda i, j: (i, j + jb))],
        out_specs=pl.BlockSpec((tm, TN), lambda i, j: (i, j)),
        out_shape=jax.ShapeDtypeStruct((R, N), BF16),
        compiler_params=_params("parallel", "parallel"),
        name="merge_proj",
    )(ya, yb, wa, wb, gates, gates)


def _split3(x):
    hi = x.astype(BF16)
    r1 = x - hi.astype(F32)
    mid = r1.astype(BF16)
    lo = (r1 - mid.astype(F32)).astype(BF16)
    return hi, mid, lo


def _lane_cumsum(x):
    n = x.shape[1]
    row = lax.broadcasted_iota(jnp.int32, (n, n), 0)
    col = lax.broadcasted_iota(jnp.int32, (n, n), 1)
    tri = jnp.where(row <= col, 1.0, 0.0).astype(BF16)
    hi, mid, lo = _split3(x)
    return _dot(hi, tri) + _dot(mid, tri) + _dot(lo, tri)


def _csum_kernel(lf_ref, o_ref, carry_s, *, n_heads, n_valid, subtract_total):
    blk = pl.program_id(1)

    @pl.when(blk == 0)
    def _():
        carry_s[...] = jnp.zeros_like(carry_s)

    lf = lf_ref[...]
    tb = lf.shape[0]
    if n_valid < tb:
        row = lax.broadcasted_iota(jnp.int32, lf.shape, 0)
        lf = jnp.where(row < n_valid, lf, 0.0)
    lt = lf.T[:n_heads]
    c = _lane_cumsum(lt) + carry_s[:, 0:1]
    last = c[:, tb - 1:tb]
    carry_s[...] = jnp.broadcast_to(last, carry_s.shape)
    if subtract_total:
        c = c - last
    o_ref[0] = c


def log_forget_cumsum(gates, f_col, row_start, n_seq, seq_len, n_heads, *, tb, n_valid=None,
                      subtract_total=False):
    n_valid = tb if n_valid is None else n_valid
    nb = seq_len // tb
    assert row_start % tb == 0 and f_col % 128 == 0
    r0, c0 = row_start // tb, f_col // 128
    return pl.pallas_call(
        functools.partial(_csum_kernel, n_heads=n_heads, n_valid=n_valid, subtract_total=subtract_total),
        grid=(n_seq, nb),
        in_specs=[pl.BlockSpec((tb, 128), lambda s, b: (r0 + s * nb + b, c0))],
        out_specs=pl.BlockSpec((1, n_heads, tb), lambda s, b: (s, 0, b)),
        out_shape=jax.ShapeDtypeStruct((n_seq, n_heads, seq_len), F32),
        scratch_shapes=[pltpu.VMEM((n_heads, 128), F32)],
        compiler_params=_params("parallel", "arbitrary"),
        name="logf_cumsum",
    )(gates)


def _online_update(carry, s, v):
    m, l, acc = carry
    m_new = jnp.maximum(m, jnp.max(s, axis=1, keepdims=True))
    alpha = jnp.exp(m - m_new)
    p = jnp.exp(s - m_new)
    l_new = alpha * l + jnp.sum(p, axis=1, keepdims=True)
    acc_new = alpha * acc + _dot(p.astype(BF16), v)
    return m_new, l_new, acc_new


def _pattn_kernel(q_ref, k_ref, v_ref, ck_ref, km_ref, vm_ref, cm_ref, o_ref, *, n_meta, blk, n_heads):
    T = q_ref.shape[0]
    scale = HEAD_DIM ** -0.5
    lane = lax.broadcasted_iota(jnp.int32, (1, km_ref.shape[0]), 1)
    row = lax.broadcasted_iota(jnp.int32, (blk, blk), 0)
    col = lax.broadcasted_iota(jnp.int32, (blk, blk), 1)
    causal = col <= row
    heads = [slice(g * HEAD_DIM, (g + 1) * HEAD_DIM) for g in range(n_heads)]

    def q_block(qi, _):
        q0 = pl.multiple_of(qi * blk, blk)
        qs = [q_ref[pl.ds(q0, blk), hs] for hs in heads]
        carry = []
        for g, hs in enumerate(heads):
            bias_m = jnp.where(lane < n_meta, -cm_ref[0, g], NEG_BIG)
            sm = _dot_nt(qs[g], km_ref[:, hs]) * scale + bias_m
            m0 = jnp.max(sm, axis=1, keepdims=True)
            p0 = jnp.exp(sm - m0)
            carry.append((m0, jnp.sum(p0, axis=1, keepdims=True), _dot(p0.astype(BF16), vm_ref[:, hs])))

        def kv_block(ki, carry):
            k0 = pl.multiple_of(ki * blk, blk)
            new = []
            for g, hs in enumerate(heads):
                s = _dot_nt(qs[g], k_ref[pl.ds(k0, blk), hs]) * scale - ck_ref[0, g, pl.ds(ki, 1), :]
                new.append(_online_update(carry[g], s, v_ref[pl.ds(k0, blk), hs]))
            return tuple(new)

        carry = lax.fori_loop(0, qi, kv_block, tuple(carry))
        for g, hs in enumerate(heads):
            s = _dot_nt(qs[g], k_ref[pl.ds(q0, blk), hs]) * scale - ck_ref[0, g, pl.ds(qi, 1), :]
            s = jnp.where(causal, s, NEG_BIG)
            _, l, acc = _online_update(carry[g], s, v_ref[pl.ds(q0, blk), hs])
            o_ref[pl.ds(q0, blk), hs] = (acc / l).astype(o_ref.dtype)
        return 0

    lax.fori_loop(0, T // blk, q_block, 0)


def prompt_attention(qkv16, c_prompt, c_meta, B, T, H, meta_row, n_meta):
    blk = _pick(T, ATT_BLOCK, 128)
    G = _pick(H, ATT_HEADS, 1)
    W = G * HEAD_DIM
    HG = H // G
    ck = c_prompt.reshape(B, H, T // blk, blk)
    cm = c_meta.reshape(1, H, 1, META_PAD)
    mrow = meta_row // META_PAD
    return pl.pallas_call(
        functools.partial(_pattn_kernel, n_meta=n_meta, blk=blk, n_heads=G),
        grid=(B, HG),
        in_specs=[pl.BlockSpec((T, W), lambda b, h: (b, h)),
                  pl.BlockSpec((T, W), lambda b, h: (b, HG + h)),
                  pl.BlockSpec((T, W), lambda b, h: (b, 2 * HG + h)),
                  pl.BlockSpec((1, G, T // blk, blk), lambda b, h: (b, h, 0, 0)),
                  pl.BlockSpec((META_PAD, W), lambda b, h: (mrow, HG + h)),
                  pl.BlockSpec((META_PAD, W), lambda b, h: (mrow, 2 * HG + h)),
                  pl.BlockSpec((1, G, 1, META_PAD), lambda b, h: (0, h, 0, 0))],
        out_specs=pl.BlockSpec((T, W), lambda b, h: (b, h)),
        out_shape=jax.ShapeDtypeStruct((B * T, H * HEAD_DIM), BF16),
        compiler_params=_params("parallel", "parallel"),
        name="prompt_attention",
    )(qkv16, qkv16, qkv16, ck, qkv16, qkv16, cm)


def _mattn_kernel(q_ref, k_ref, v_ref, cm_ref, o_ref, *, n_meta):
    scale = HEAD_DIM ** -0.5
    n = q_ref.shape[0]
    row = lax.broadcasted_iota(jnp.int32, (n, n), 0)
    col = lax.broadcasted_iota(jnp.int32, (n, n), 1)
    visible = jnp.logical_and(col < n_meta, jnp.logical_or(col <= row, row >= n_meta))
    s = _dot_nt(q_ref[...], k_ref[...]) * scale - cm_ref[0, 0]
    s = jnp.where(visible, s, NEG_BIG)
    p = jnp.exp(s - jnp.max(s, axis=1, keepdims=True))
    l = jnp.sum(p, axis=1, keepdims=True)
    o_ref[...] = (_dot(p.astype(BF16), v_ref[...]) / l).astype(o_ref.dtype)


def meta_attention(qkv16, c_meta, H, meta_row, n_meta):
    mrow = meta_row // META_PAD
    cm = c_meta.reshape(1, H, 1, META_PAD)
    return pl.pallas_call(
        functools.partial(_mattn_kernel, n_meta=n_meta),
        grid=(H,),
        in_specs=[pl.BlockSpec((META_PAD, HEAD_DIM), lambda h: (mrow, h)),
                  pl.BlockSpec((META_PAD, HEAD_DIM), lambda h: (mrow, H + h)),
                  pl.BlockSpec((META_PAD, HEAD_DIM), lambda h: (mrow, 2 * H + h)),
                  pl.BlockSpec((1, 1, 1, META_PAD), lambda h: (0, h, 0, 0))],
        out_specs=pl.BlockSpec((META_PAD, HEAD_DIM), lambda h: (0, h)),
        out_shape=jax.ShapeDtypeStruct((META_PAD, H * HEAD_DIM), BF16),
        compiler_params=_params("parallel"),
        name="meta_attention",
    )(qkv16, qkv16, qkv16, cm)


def _dattn_kernel(pt_ref, q_ref, kn_ref, vn_ref, lfn_ref, *refs, n_steps, g):
    del pt_ref
    kp_refs, vp_refs, lfp_refs = refs[:g], refs[g:2 * g], refs[2 * g:3 * g]
    o_ref, m_s, l_s, acc_s, c_s = refs[3 * g:]
    step = pl.program_id(1)
    H = q_ref.shape[1]
    n_keys = kp_refs[0].shape[1]
    tiles = lfp_refs[0].shape[1]
    n = g * tiles
    scale = HEAD_DIM ** -0.5

    @pl.when(step == 0)
    def _():
        m_s[...] = jnp.full_like(m_s, NEG_BIG)
        l_s[...] = jnp.zeros_like(l_s)
        acc_s[...] = jnp.zeros_like(acc_s)
        c_s[...] = jnp.zeros_like(c_s)

    li = lax.broadcasted_iota(jnp.int32, (128, 128), 0)
    lj = lax.broadcasted_iota(jnp.int32, (128, 128), 1)
    same_head = (li % H) == (lj % H)
    w_within = jnp.where(jnp.logical_and(same_head, li // H <= lj // H), 1.0, 0.0)
    w_total = jnp.where(same_head, 1.0, 0.0)
    w = jnp.concatenate([w_within, w_total], axis=1).astype(BF16)
    lf = jnp.concatenate([r[0] for r in lfp_refs], axis=0)
    parts = _dot(jnp.concatenate(_split3(lf), axis=0), w)
    both = parts[:n] + parts[n:2 * n] + parts[2 * n:]
    within, total = both[:, :128], both[:, 128:]
    ri = lax.broadcasted_iota(jnp.int32, (n, n), 0)
    rj = lax.broadcasted_iota(jnp.int32, (n, n), 1)
    strict = jnp.where(rj < ri, 1.0, 0.0).astype(BF16)
    before = _dot(jnp.concatenate([strict] * 3, axis=1), jnp.concatenate(_split3(total), axis=0))
    carry = c_s[0:1, :]
    c_all = within + before + carry
    carry_new = before[n - 1:n] + total[n - 1:n] + carry
    c_s[...] = jnp.broadcast_to(carry_new, c_s.shape)

    q = q_ref[0]
    lane = lax.broadcasted_iota(jnp.int32, (H, 128), 1)
    sub = lax.broadcasted_iota(jnp.int32, (H, 128), 0)
    own_head = (lane % H) == sub
    pieces = []
    for i in range(g):
        s_i = _dot_nt(q, kp_refs[i][0].astype(BF16)) * scale
        for j in range(tiles):
            t = s_i[:, j * 128:(j + 1) * 128] - c_all[i * tiles + j:i * tiles + j + 1, :]
            pieces.append(jnp.where(own_head, t, NEG_BIG))
    s = jnp.concatenate(pieces, axis=1)

    m, l = m_s[:, 0:1], l_s[:, 0:1]
    m_new = jnp.maximum(m, jnp.max(s, axis=1, keepdims=True))
    alpha = jnp.exp(m - m_new)
    p = jnp.exp(s - m_new)
    l_new = alpha * l + jnp.sum(p, axis=1, keepdims=True)
    pv = _dot(p[:, :n_keys].astype(BF16), vp_refs[0][0].astype(BF16))
    for i in range(1, g):
        pv = pv + _dot(p[:, i * n_keys:(i + 1) * n_keys].astype(BF16), vp_refs[i][0].astype(BF16))
    acc = alpha * acc_s[...] + pv
    m_s[...] = jnp.broadcast_to(m_new, m_s.shape)
    l_s[...] = jnp.broadcast_to(l_new, l_s.shape)
    acc_s[...] = acc

    @pl.when(step == n_steps - 1)
    def _():
        c_past = jnp.sum(jnp.where(lane == sub, jnp.broadcast_to(carry_new, (H, 128)), 0.0), axis=1, keepdims=True)
        c_new = c_past + lfn_ref[0]
        s_new = jnp.sum(q.astype(F32) * kn_ref[0].astype(F32), axis=1, keepdims=True) * scale - c_new
        m2 = jnp.maximum(m_new, s_new)
        a2 = jnp.exp(m_new - m2)
        p_new = jnp.exp(s_new - m2)
        l2 = a2 * l_new + p_new
        o_ref[0] = ((a2 * acc + p_new * vn_ref[0].astype(F32)) / l2).astype(o_ref.dtype)


def decode_attention(page_table, q, k_new, v_new, lf_new, cache_k, cache_v, cache_lf):
    DB, H, _ = q.shape
    n_pages = page_table.shape[1]
    n_keys = cache_k.shape[1]
    tiles = cache_lf.shape[1]
    assert 128 % H == 0 and tiles * 128 == n_keys
    g = _pick(n_pages, DEC_PAGES, 1)
    n_steps = n_pages // g

    def page_map(i):
        return lambda b, p, pt: (pt[b * n_pages + p * g + i], 0, 0)

    per_seq = lambda b, p, pt: (b, 0, 0)
    grid_spec = pltpu.PrefetchScalarGridSpec(
        num_scalar_prefetch=1,
        grid=(DB, n_steps),
        in_specs=([pl.BlockSpec((1, H, HEAD_DIM), per_seq)] * 3 + [pl.BlockSpec((1, H, 1), per_seq)]
                  + [pl.BlockSpec((1, n_keys, HEAD_DIM), page_map(i)) for i in range(g)]
                  + [pl.BlockSpec((1, n_keys, HEAD_DIM), page_map(i)) for i in range(g)]
                  + [pl.BlockSpec((1, tiles, 128), page_map(i)) for i in range(g)]),
        out_specs=pl.BlockSpec((1, H, HEAD_DIM), per_seq),
        scratch_shapes=[pltpu.VMEM((H, 128), F32), pltpu.VMEM((H, 128), F32),
                        pltpu.VMEM((H, HEAD_DIM), F32), pltpu.VMEM((8, 128), F32)],
    )
    return pl.pallas_call(
        functools.partial(_dattn_kernel, n_steps=n_steps, g=g),
        grid_spec=grid_spec,
        out_shape=jax.ShapeDtypeStruct((DB, H, HEAD_DIM), BF16),
        compiler_params=_params("parallel", "arbitrary"),
        name="decode_attention",
    )(page_table.reshape(-1), q, k_new, v_new, lf_new, *([cache_k] * g), *([cache_v] * g), *([cache_lf] * g))


def _lru_gates(xc, wa_ref, ba, wx_ref, bx, lam):
    xcb = xc.astype(BF16)
    nb = xc.shape[1] // LRU_BLOCK
    ra = jnp.concatenate([_dot(xcb[:, n * LRU_BLOCK:(n + 1) * LRU_BLOCK], wa_ref[n]) for n in range(nb)], axis=1)
    rx = jnp.concatenate([_dot(xcb[:, n * LRU_BLOCK:(n + 1) * LRU_BLOCK], wx_ref[n]) for n in range(nb)], axis=1)
    r = jax.nn.sigmoid(ra + ba)
    i = jax.nn.sigmoid(rx + bx)
    log_a = (-LRU_C) * r * _softplus(-lam)
    a = jnp.exp(log_a)
    u = jnp.sqrt(-jnp.tanh(log_a) * (a * a + 1.0)) * (i * xc)
    return a, u


def _rnn_kernel(xb_ref, yb_ref, h0_ref, c0_ref, cw_ref, cb_ref, wa_ref, ba_ref, wx_ref, bx_ref, lam_ref,
                y_ref, hl_ref, xprev_s, h_s, a_s, u_s):
    @pl.when(pl.program_id(1) == 0)
    def _():
        xprev_s[...] = c0_ref[0]
        h_s[...] = h0_ref[0]

    xb = xb_ref[...]
    n = xb.shape[0]
    xp = jnp.concatenate([xprev_s[...], xb], axis=0)
    xc = cb_ref[...] + pltpu.roll(xp, 3, 0)[8:] * cw_ref[0:1, :]
    xc = xc + pltpu.roll(xp, 2, 0)[8:] * cw_ref[1:2, :]
    xc = xc + pltpu.roll(xp, 1, 0)[8:] * cw_ref[2:3, :]
    xc = xc + xb * cw_ref[3:4, :]
    a, u = _lru_gates(xc, wa_ref, ba_ref[...], wx_ref, bx_ref[...], lam_ref[...])
    a_s[...] = a
    u_s[...] = u

    def step(t, h):
        h = a_s[pl.ds(t, 1), :] * h + u_s[pl.ds(t, 1), :]
        u_s[pl.ds(t, 1), :] = h
        return h

    h = lax.fori_loop(0, n, step, h_s[0:1, :], unroll=8)
    h_s[...] = jnp.broadcast_to(h, h_s.shape)
    hl_ref[0] = h_s[...]
    xprev_s[...] = xb[n - 8:, :]
    y_ref[...] = (u_s[...] * jax.nn.gelu(yb_ref[...])).astype(y_ref.dtype)


def rnn_sequences(gates, xb_col, yb_col, row_start, n_seq, seq_len, chunk, h0, c0, lru):
    cw, cb, wa, ba, wx, bx, lam = lru
    C = cw.shape[1]
    nb = C // LRU_BLOCK
    nc = seq_len // chunk
    assert row_start % chunk == 0 and xb_col % C == 0 and yb_col % C == 0
    r0, jx, jy = row_start // chunk, xb_col // C, yb_col // C
    vec = pl.BlockSpec((1, C), lambda s, c: (0, 0))
    return pl.pallas_call(
        _rnn_kernel,
        grid=(n_seq, nc),
        in_specs=[pl.BlockSpec((chunk, C), lambda s, c: (r0 + s * nc + c, jx)),
                  pl.BlockSpec((chunk, C), lambda s, c: (r0 + s * nc + c, jy)),
                  pl.BlockSpec((1, 8, C), lambda s, c: (s, 0, 0)),
                  pl.BlockSpec((1, 8, C), lambda s, c: (s, 0, 0)),
                  pl.BlockSpec((cw.shape[0], C), lambda s, c: (0, 0)),
                  vec,
                  pl.BlockSpec((nb, LRU_BLOCK, LRU_BLOCK), lambda s, c: (0, 0, 0)),
                  vec,
                  pl.BlockSpec((nb, LRU_BLOCK, LRU_BLOCK), lambda s, c: (0, 0, 0)),
                  vec, vec],
        out_specs=[pl.BlockSpec((chunk, C), lambda s, c: (s * nc + c, 0)),
                   pl.BlockSpec((1, 8, C), lambda s, c: (s, 0, 0))],
        out_shape=[jax.ShapeDtypeStruct((n_seq * seq_len, C), BF16),
                   jax.ShapeDtypeStruct((n_seq, 8, C), F32)],
        scratch_shapes=[pltpu.VMEM((8, C), F32), pltpu.VMEM((8, C), F32),
                        pltpu.VMEM((chunk, C), F32), pltpu.VMEM((chunk, C), F32)],
        compiler_params=_params("parallel", "arbitrary"),
        name="rnn_sequences",
    )(gates, gates, h0, c0, cw, cb, wa, ba, wx, bx, lam)


def _rnn_step_kernel(xb_ref, yb_ref, h0_ref, s0_ref, s1_ref, s2_ref, cw_ref, cb_ref, wa_ref, ba_ref,
                     wx_ref, bx_ref, lam_ref, y_ref, h_ref):
    xb = xb_ref[...]
    xc = cb_ref[...] + s0_ref[...] * cw_ref[0:1, :]
    xc = xc + s1_ref[...] * cw_ref[1:2, :]
    xc = xc + s2_ref[...] * cw_ref[2:3, :]
    xc = xc + xb * cw_ref[3:4, :]
    a, u = _lru_gates(xc, wa_ref, ba_ref[...], wx_ref, bx_ref[...], lam_ref[...])
    h = a * h0_ref[...] + u
    h_ref[...] = h
    y_ref[...] = (h * jax.nn.gelu(yb_ref[...])).astype(y_ref.dtype)


def rnn_step(gates, xb_col, yb_col, row_start, n_rows, h0, s0, s1, s2, lru):
    cw, cb, wa, ba, wx, bx, lam = lru
    C = cw.shape[1]
    nb = C // LRU_BLOCK
    assert row_start % n_rows == 0
    r0, jx, jy = row_start // n_rows, xb_col // C, yb_col // C
    full = pl.BlockSpec((n_rows, C), lambda i: (0, 0))
    vec = pl.BlockSpec((1, C), lambda i: (0, 0))
    return pl.pallas_call(
        _rnn_step_kernel,
        grid=(1,),
        in_specs=[pl.BlockSpec((n_rows, C), lambda i: (r0, jx)),
                  pl.BlockSpec((n_rows, C), lambda i: (r0, jy)),
                  full, full, full, full,
                  pl.BlockSpec((cw.shape[0], C), lambda i: (0, 0)),
                  vec,
                  pl.BlockSpec((nb, LRU_BLOCK, LRU_BLOCK), lambda i: (0, 0, 0)),
                  vec,
                  pl.BlockSpec((nb, LRU_BLOCK, LRU_BLOCK), lambda i: (0, 0, 0)),
                  vec, vec],
        out_specs=[full, full],
        out_shape=[jax.ShapeDtypeStruct((n_rows, C), BF16), jax.ShapeDtypeStruct((n_rows, C), F32)],
        compiler_params=_params("arbitrary"),
        name="rnn_step",
    )(gates, gates, h0, s0, s1, s2, cw, cb, wa, ba, wx, bx, lam)


def _half_ffn(x, g, w1, w3, w2, tm):
    D, FF = w1.shape
    ffp = -(-FF // 1024) * 1024
    w1p = jnp.concatenate([w1.astype(BF16), jnp.zeros((D, ffp - FF), BF16)], axis=1)
    w3p = jnp.concatenate([w3.astype(BF16), jnp.zeros((D, ffp - FF), BF16)], axis=1)
    w2p = jnp.concatenate([w2.astype(BF16), jnp.zeros((ffp - FF, D), BF16)], axis=0)
    h = rmsnorm(x, g, BF16)
    act = ffn_up(h, w1p, w3p, tm)
    return mm_residual(act, w2p, x, 0.5, tm, _pick(D, 1024, 128), _pick(ffp, TK_TARGET, 256))


def kernel(x_prompt, x_sample, cache_k, cache_v, cache_logf, state_h, state_conv, page_table, meta,
           norm_ffn1, ffn1_w1, ffn1_w3, ffn1_w2, norm_mix, w_in, b_f, b_gate, conv_w, conv_b,
           lru_wa, lru_ba, lru_wx, lru_bx, lru_lambda, w_pa, w_pb, w_o,
           norm_ffn2, ffn2_w1, ffn2_w3, ffn2_w2, norm_final):
    B, T, D = x_prompt.shape
    DB, dec_seq, _ = x_sample.shape
    depth, n_pool, page, H, hd = cache_k.shape
    n_meta = meta.shape[0]
    C = state_h.shape[-1]
    conv_width = conv_w.shape[1]
    DA = H * hd
    assert depth == 1 and dec_seq == 1 and hd == HEAD_DIM and conv_width == 4
    assert n_meta <= META_PAD and DB % 128 == 0 and page == 128
    assert w_in.shape[-1] == 3 * DA + H + 2 * C + 2 * D

    Rm = B * T
    s_row, m_row = Rm, Rm + DB
    R = Rm + DB + META_PAD
    tm = _pick(R, TM_TARGET, 16)

    x0 = jnp.concatenate([x_prompt.reshape(Rm, D), x_sample.reshape(DB, D), meta.astype(F32),
                          jnp.zeros((META_PAD - n_meta, D), F32)], axis=0)

    x1 = _half_ffn(x0, norm_ffn1[0], ffn1_w1[0], ffn1_w3[0], ffn1_w2[0], tm)

    h2 = rmsnorm(x1, norm_mix[0], BF16)
    w = w_in[0]
    o_f = 3 * DA
    o_x = o_f + H
    w_qkv = w[:, :o_f].astype(BF16)
    f_pad = TN - H
    w_rest = lax.optimization_barrier(
        jnp.concatenate([w[:, o_x:], w[:, o_f:o_x], jnp.zeros((D, f_pad), F32)], axis=1)).astype(BF16)
    bias = jnp.concatenate([jnp.zeros((2 * C,), F32), b_gate[0], b_f[0], jnp.zeros((f_pad,), F32)]).reshape(1, -1)
    qkv32, qkv16 = mm_dual(h2, w_qkv, tm)
    gates = mm_gate(h2, w_rest, bias, tm, 2 * C, 2 * D)
    col_ga, col_gb, col_f = 2 * C, 2 * C + D, 2 * C + 2 * D

    c_prompt = log_forget_cumsum(gates, col_f, 0, B, T, H, tb=_pick(T, 256, 128))
    c_meta = log_forget_cumsum(gates, col_f, m_row, 1, META_PAD, H, tb=META_PAD, n_valid=n_meta,
                               subtract_total=True)
    y_att_p = prompt_attention(qkv16, c_prompt, c_meta, B, T, H, m_row, n_meta)
    y_att_m = meta_attention(qkv16, c_meta, H, m_row, n_meta)
    samp16 = qkv16[s_row:s_row + DB]
    heads = lambda t: t.reshape(DB, H, HEAD_DIM)
    lf_s = gates[s_row:s_row + DB, col_f:col_f + H]
    y_att_s = decode_attention(
        page_table, heads(samp16[:, :DA]), heads(samp16[:, DA:2 * DA]), heads(samp16[:, 2 * DA:]),
        lf_s.reshape(DB, H, 1), cache_k.reshape(n_pool, page * H, HEAD_DIM),
        cache_v.reshape(n_pool, page * H, HEAD_DIM), cache_logf.reshape(n_pool, page * H // 128, 128))
    y_att = jnp.concatenate([y_att_p, y_att_s.reshape(DB, DA), y_att_m], axis=0)

    lru = (conv_w[0], conv_b[0].reshape(1, C), lru_wa[0].astype(BF16), lru_ba[0].reshape(1, C),
           lru_wx[0].astype(BF16), lru_bx[0].reshape(1, C), lru_lambda[0].reshape(1, C))
    n_meta8 = -(-n_meta // 8) * 8
    assert n_meta8 == n_meta and n_meta >= 8
    zeros8 = jnp.zeros((1, 8, C), F32)
    y_rnn_m, h_meta = rnn_sequences(gates, 0, C, m_row, 1, n_meta, n_meta, zeros8, zeros8, lru)
    xb_meta_tail = gates[m_row + n_meta - 8:m_row + n_meta, :C]
    y_rnn_p, h_prompt = rnn_sequences(
        gates, 0, C, 0, B, T, _pick(T, RNN_CHUNK, 8),
        jnp.broadcast_to(h_meta, (B, 8, C)), jnp.broadcast_to(xb_meta_tail[None], (B, 8, C)), lru)
    sc = state_conv[0]
    y_rnn_s, h_sample = rnn_step(gates, 0, C, s_row, DB, state_h[0], sc[:, 0], sc[:, 1], sc[:, 2], lru)
    y_rnn = jnp.concatenate([y_rnn_p, y_rnn_s, y_rnn_m, jnp.zeros((META_PAD - n_meta, C), BF16)], axis=0)

    merged = merge_proj(y_att, y_rnn, w_pa[0].astype(BF16), w_pb[0].astype(BF16), gates, col_ga, col_gb, tm)
    x2 = mm_residual(merged, w_o[0].astype(BF16), x1, 1.0, tm, TN, D)
    x3 = _half_ffn(x2, norm_ffn2[0], ffn2_w1[0], ffn2_w3[0], ffn2_w2[0], tm)

    y_prompt = rmsnorm(x3, norm_final, F32, row_start=0, n_rows=Rm).reshape(B, T, D)
    y_sample = rmsnorm(x3, norm_final, F32, row_start=s_row, n_rows=DB).reshape(DB, 1, D)

    def with_meta(full, lo, hi, tail):
        real = full[:Rm, lo:hi].reshape((B, T) + tail)
        m = jnp.broadcast_to(full[m_row:m_row + n_meta, lo:hi].reshape((1, n_meta) + tail), (B, n_meta) + tail)
        return jnp.concatenate([m, real], axis=1)[None]

    new_k_p = with_meta(qkv32, DA, 2 * DA, (H, HEAD_DIM))
    new_v_p = with_meta(qkv32, 2 * DA, 3 * DA, (H, HEAD_DIM))
    new_lf_p = with_meta(gates, col_f, col_f + H, (H,))
    new_h_p = h_prompt[:, 0][None]
    xb_p = gates[:Rm, :C].reshape(B, T, C)
    new_conv_p = xb_p[:, T - (conv_width - 1):][None]
    samp32 = qkv32[s_row:s_row + DB]
    new_k_s = samp32[:, DA:2 * DA].reshape(1, DB, 1, H, HEAD_DIM)
    new_v_s = samp32[:, 2 * DA:].reshape(1, DB, 1, H, HEAD_DIM)
    new_lf_s = lf_s.reshape(1, DB, 1, H)
    new_h_s = h_sample[None]
    new_conv_s = jnp.concatenate([sc[:, 1:], gates[s_row:s_row + DB, :C][:, None]], axis=1)[None]
    return (y_prompt, y_sample, new_k_p, new_v_p, new_lf_p, new_h_p, new_conv_p,
            new_k_s, new_v_s, new_lf_s, new_h_s, new_conv_s)
```

```python
import functools

import jax
import jax.numpy as jnp
from jax import lax
from jax.experimental import pallas as pl
from jax.experimental.pallas import tpu as pltpu

F32 = jnp.float32
BF16 = jnp.bfloat16

NORM_EPS = 1e-6
LRU_C = 8.0
HEAD_DIM = 128
LRU_BLOCK = 128
META_PAD = 128
NEG_BIG = -1e30
VMEM_LIMIT_BYTES = 56 * 1024 * 1024
TM_TARGET = 1056
TN = 512
TK_TARGET = 2816
ATT_BLOCK = 1024
ATT_HEADS = 2
DEC_PAGES = 4
RNN_CHUNK = 256


def _pick(n, target, mult):
    best = None
    for d in range(mult, min(n, target) + 1, mult):
        if n % d == 0:
            best = d
    assert best is not None, (n, target, mult)
    return best


def _params(*sem):
    return pltpu.CompilerParams(dimension_semantics=sem, vmem_limit_bytes=VMEM_LIMIT_BYTES)


def _dot(a, b):
    return jnp.dot(a, b, preferred_element_type=F32)


def _dot_nt(a, b):
    return lax.dot_general(a, b, (((1,), (1,)), ((), ())), preferred_element_type=F32)


def _softplus(x):
    return jnp.maximum(x, 0.0) + jnp.log1p(jnp.exp(-jnp.abs(x)))


def _rmsnorm_kernel(x_ref, g_ref, o_ref):
    x = x_ref[...]
    ms = jnp.mean(x * x, axis=-1, keepdims=True)
    o_ref[...] = ((x * lax.rsqrt(ms + NORM_EPS)) * g_ref[...]).astype(o_ref.dtype)


def rmsnorm(x, g, out_dtype, *, row_start=0, n_rows=None):
    R, D = x.shape
    n_rows = R if n_rows is None else n_rows
    tr = _pick(n_rows, 264, 16)
    assert row_start % tr == 0
    off = row_start // tr
    return pl.pallas_call(
        _rmsnorm_kernel,
        grid=(n_rows // tr,),
        in_specs=[pl.BlockSpec((tr, D), lambda i: (i + off, 0)),
                  pl.BlockSpec((1, D), lambda i: (0, 0))],
        out_specs=pl.BlockSpec((tr, D), lambda i: (i, 0)),
        out_shape=jax.ShapeDtypeStruct((n_rows, D), out_dtype),
        compiler_params=_params("parallel"),
        name="rmsnorm",
    )(x, g.reshape(1, D))


def _cast_pad_kernel(*refs, axis, n_full, has_tail):
    a_ref, o_ref = refs[0], refs[-1]
    j = pl.program_id(axis)

    @pl.when(j < n_full)
    def _():
        o_ref[...] = a_ref[...].astype(BF16)

    if has_tail:
        b_ref = refs[1]

        @pl.when(j == n_full)
        def _():
            tail = b_ref[...].astype(BF16)
            pad = list(o_ref.shape)
            pad[axis] -= tail.shape[axis]
            o_ref[...] = jnp.concatenate([tail, jnp.zeros(pad, BF16)], axis=axis)

    @pl.when(j >= n_full + (1 if has_tail else 0))
    def _():
        o_ref[...] = jnp.zeros(o_ref.shape, BF16)


def cast_pad(w, axis, n_out):
    blk = [512, 512]
    n_in = w.shape[axis]
    n_full, rem = divmod(n_in, blk[axis])
    assert n_out % blk[axis] == 0 and w.shape[1 - axis] % 512 == 0 and n_out >= n_in
    has_tail = rem > 0
    last = max(n_full - 1, 0)

    def a_map(i, j):
        idx = [i, j]
        idx[axis] = jnp.minimum(idx[axis], last)
        return tuple(idx)

    operands, in_specs = [w], [pl.BlockSpec(tuple(blk), a_map)]
    if has_tail:
        assert (n_in - rem) % rem == 0 and rem % 128 == 0
        tail_blk = list(blk)
        tail_blk[axis] = rem
        tail_idx = (n_in - rem) // rem

        def b_map(i, j):
            idx = [i, j]
            idx[axis] = tail_idx
            return tuple(idx)

        operands.append(w)
        in_specs.append(pl.BlockSpec(tuple(tail_blk), b_map))
    out_shape = list(w.shape)
    out_shape[axis] = n_out
    return pl.pallas_call(
        functools.partial(_cast_pad_kernel, axis=axis, n_full=n_full, has_tail=has_tail),
        grid=(out_shape[0] // 512, out_shape[1] // 512),
        in_specs=in_specs,
        out_specs=pl.BlockSpec((512, 512), lambda i, j: (i, j)),
        out_shape=jax.ShapeDtypeStruct(tuple(out_shape), BF16),
        compiler_params=_params("parallel", "parallel"),
        name="cast_pad",
    )(*operands)


def _regroup_kernel(a_ref, b_ref, o_ref, *, n_main_blocks, shift):
    j = pl.program_id(1)
    a = a_ref[...]
    width = a.shape[1]
    lane = lax.broadcasted_iota(jnp.int32, a.shape, 1)

    @pl.when(j < n_main_blocks)
    def _():
        ra = pltpu.roll(a, width - shift, 1)
        rb = pltpu.roll(b_ref[...], width - shift, 1)
        o_ref[...] = jnp.where(lane < width - shift, ra, rb).astype(BF16)

    @pl.when(j == n_main_blocks)
    def _():
        o_ref[...] = jnp.where(lane < shift, a, 0.0).astype(BF16)


def regroup_in_proj(w, col0, shift, n_main):
    D, NW = w.shape
    assert col0 % 512 == 0 and n_main % 512 == 0 and 0 < shift < 128 and D % 512 == 0
    assert NW == col0 + shift + n_main
    c0 = col0 // 512
    nmb = n_main // 512
    return pl.pallas_call(
        functools.partial(_regroup_kernel, n_main_blocks=nmb, shift=shift),
        grid=(D // 512, nmb + 1),
        in_specs=[pl.BlockSpec((512, 512), lambda i, j: (i, c0 + jnp.where(j < nmb, j, 0))),
                  pl.BlockSpec((512, 512), lambda i, j: (i, c0 + 1 + jnp.minimum(j, nmb - 1)))],
        out_specs=pl.BlockSpec((512, 512), lambda i, j: (i, j)),
        out_shape=jax.ShapeDtypeStruct((D, n_main + 512), BF16),
        compiler_params=_params("parallel", "parallel"),
        name="regroup_in_proj",
    )(w, w)


def _ffn_up_kernel(h_ref, w1_ref, w3_ref, o_ref):
    h = h_ref[...]
    a = _dot(h, w1_ref[...])
    b = _dot(h, w3_ref[...])
    o_ref[...] = (a * jax.nn.sigmoid(a) * b).astype(o_ref.dtype)


def ffn_up(h, w1, w3, tm):
    R, K = h.shape
    N = w1.shape[1]
    return pl.pallas_call(
        _ffn_up_kernel,
        grid=(R // tm, N // TN),
        in_specs=[pl.BlockSpec((tm, K), lambda i, j: (i, 0)),
                  pl.BlockSpec((K, TN), lambda i, j: (0, j)),
                  pl.BlockSpec((K, TN), lambda i, j: (0, j))],
        out_specs=pl.BlockSpec((tm, TN), lambda i, j: (i, j)),
        out_shape=jax.ShapeDtypeStruct((R, N), BF16),
        compiler_params=_params("parallel", "parallel"),
        name="ffn_up",
    )(h, w1, w3)


def _mm_res_kernel(a_ref, w_ref, res_ref, o_ref, *, scale, nk):
    part = _dot(a_ref[...], w_ref[...])
    if nk == 1:
        o_ref[...] = res_ref[...] + scale * part
        return
    k = pl.program_id(2)

    @pl.when(k == 0)
    def _():
        o_ref[...] = part

    @pl.when(jnp.logical_and(k > 0, k < nk - 1))
    def _():
        o_ref[...] += part

    @pl.when(k == nk - 1)
    def _():
        o_ref[...] = res_ref[...] + scale * (o_ref[...] + part)


def mm_residual(a, w, res, scale, tm, tn, tk):
    R, K = a.shape
    N = w.shape[1]
    nk = K // tk
    return pl.pallas_call(
        functools.partial(_mm_res_kernel, scale=scale, nk=nk),
        grid=(R // tm, N // tn, nk),
        in_specs=[pl.BlockSpec((tm, tk), lambda i, j, k: (i, k)),
                  pl.BlockSpec((tk, tn), lambda i, j, k: (k, j)),
                  pl.BlockSpec((tm, tn), lambda i, j, k: (i, j))],
        out_specs=pl.BlockSpec((tm, tn), lambda i, j, k: (i, j)),
        out_shape=jax.ShapeDtypeStruct((R, N), F32),
        compiler_params=_params("parallel", "parallel", "arbitrary"),
        name="mm_residual",
    )(a, w, res)


def _mm_dual_kernel(a_ref, w_ref, o32_ref, o16_ref):
    acc = _dot(a_ref[...], w_ref[...])
    o32_ref[...] = acc
    o16_ref[...] = acc.astype(BF16)


def mm_dual(a, w, tm):
    R, K = a.shape
    N = w.shape[1]
    return pl.pallas_call(
        _mm_dual_kernel,
        grid=(R // tm, N // TN),
        in_specs=[pl.BlockSpec((tm, K), lambda i, j: (i, 0)),
                  pl.BlockSpec((K, TN), lambda i, j: (0, j))],
        out_specs=[pl.BlockSpec((tm, TN), lambda i, j: (i, j)),
                   pl.BlockSpec((tm, TN), lambda i, j: (i, j))],
        out_shape=[jax.ShapeDtypeStruct((R, N), F32), jax.ShapeDtypeStruct((R, N), BF16)],
        compiler_params=_params("parallel", "parallel"),
        name="mm_qkv",
    )(a, w)


def _mm_gate_kernel(a_ref, w_ref, b_ref, o_ref, *, j_sig, j_lsig):
    j = pl.program_id(1)
    acc = _dot(a_ref[...], w_ref[...])

    @pl.when(j < j_sig)
    def _():
        o_ref[...] = acc

    @pl.when(jnp.logical_and(j >= j_sig, j < j_lsig))
    def _():
        o_ref[...] = jax.nn.sigmoid(acc + b_ref[...])

    @pl.when(j >= j_lsig)
    def _():
        o_ref[...] = -_softplus(-(acc + b_ref[...]))


def mm_gate(a, w, bias, tm, n_plain, n_sig):
    R, K = a.shape
    N = w.shape[1]
    return pl.pallas_call(
        functools.partial(_mm_gate_kernel, j_sig=n_plain // TN, j_lsig=(n_plain + n_sig) // TN),
        grid=(R // tm, N // TN),
        in_specs=[pl.BlockSpec((tm, K), lambda i, j: (i, 0)),
                  pl.BlockSpec((K, TN), lambda i, j: (0, j)),
                  pl.BlockSpec((1, TN), lambda i, j: (0, j))],
        out_specs=pl.BlockSpec((tm, TN), lambda i, j: (i, j)),
        out_shape=jax.ShapeDtypeStruct((R, N), F32),
        compiler_params=_params("parallel", "parallel"),
        name="mm_gate",
    )(a, w, bias)


def _merge_kernel(ya_ref, yb_ref, wa_ref, wb_ref, ga_ref, gb_ref, o_ref):
    pa = _dot(ya_ref[...], wa_ref[...])
    pb = _dot(yb_ref[...], wb_ref[...])
    o_ref[...] = (ga_ref[...] * pa + gb_ref[...] * pb).astype(o_ref.dtype)


def merge_proj(ya, yb, wa, wb, gates, ga_col, gb_col, tm):
    R, Ka = ya.shape
    Kb = yb.shape[1]
    N = wa.shape[1]
    ja, jb = ga_col // TN, gb_col // TN
    return pl.pallas_call(
        _merge_kernel,
        grid=(R // tm, N // TN),
        in_specs=[pl.BlockSpec((tm, Ka), lambda i, j: (i, 0)),
                  pl.BlockSpec((tm, Kb), lambda i, j: (i, 0)),
                  pl.BlockSpec((Ka, TN), lambda i, j: (0, j)),
                  pl.BlockSpec((Kb, TN), lambda i, j: (0, j)),
                  pl.BlockSpec((tm, TN), lambda i, j: (i, j + ja)),
                  pl.BlockSpec((tm, TN), lambda i, j: (i, j + jb))],
        out_specs=pl.BlockSpec((tm, TN), lambda i, j: (i, j)),
        out_shape=jax.ShapeDtypeStruct((R, N), BF16),
        compiler_params=_params("parallel", "parallel"),
        name="merge_proj",
    )(ya, yb, wa, wb, gates, gates)


def _split3(x):
    hi = x.astype(BF16)
    r1 = x - hi.astype(F32)
    mid = r1.astype(BF16)
    lo = (r1 - mid.astype(F32)).astype(BF16)
    return hi, mid, lo


def _lane_cumsum(x):
    n = x.shape[1]
    row = lax.broadcasted_iota(jnp.int32, (n, n), 0)
    col = lax.broadcasted_iota(jnp.int32, (n, n), 1)
    tri = jnp.where(row <= col, 1.0, 0.0).astype(BF16)
    hi, mid, lo = _split3(x)
    return _dot(hi, tri) + _dot(mid, tri) + _dot(lo, tri)


def _csum_kernel(lf_ref, o_ref, carry_s, *, n_heads, n_valid, subtract_total):
    blk = pl.program_id(1)

    @pl.when(blk == 0)
    def _():
        carry_s[...] = jnp.zeros_like(carry_s)

    lf = lf_ref[...]
    tb = lf.shape[0]
    if n_valid < tb:
        row = lax.broadcasted_iota(jnp.int32, lf.shape, 0)
        lf = jnp.where(row < n_valid, lf, 0.0)
    lt = lf.T[:n_heads]
    c = _lane_cumsum(lt) + carry_s[:, 0:1]
    last = c[:, tb - 1:tb]
    carry_s[...] = jnp.broadcast_to(last, carry_s.shape)
    if subtract_total:
        c = c - last
    o_ref[0] = c


def log_forget_cumsum(gates, f_col, row_start, n_seq, seq_len, n_heads, *, tb, n_valid=None,
                      subtract_total=False):
    n_valid = tb if n_valid is None else n_valid
    nb = seq_len // tb
    assert row_start % tb == 0 and f_col % 128 == 0
    r0, c0 = row_start // tb, f_col // 128
    return pl.pallas_call(
        functools.partial(_csum_kernel, n_heads=n_heads, n_valid=n_valid, subtract_total=subtract_total),
        grid=(n_seq, nb),
        in_specs=[pl.BlockSpec((tb, 128), lambda s, b: (r0 + s * nb + b, c0))],
        out_specs=pl.BlockSpec((1, n_heads, tb), lambda s, b: (s, 0, b)),
        out_shape=jax.ShapeDtypeStruct((n_seq, n_heads, seq_len), F32),
        scratch_shapes=[pltpu.VMEM((n_heads, 128), F32)],
        compiler_params=_params("parallel", "arbitrary"),
        name="logf_cumsum",
    )(gates)


def _online_update(carry, s, v):
    m, l, acc = carry
    m_new = jnp.maximum(m, jnp.max(s, axis=1, keepdims=True))
    alpha = jnp.exp(m - m_new)
    p = jnp.exp(s - m_new)
    l_new = alpha * l + jnp.sum(p, axis=1, keepdims=True)
    acc_new = alpha * acc + _dot(p.astype(BF16), v)
    return m_new, l_new, acc_new


def _pattn_kernel(q_ref, k_ref, v_ref, ck_ref, km_ref, vm_ref, cm_ref, o_ref, *, n_meta, blk, n_heads):
    T = q_ref.shape[0]
    scale = HEAD_DIM ** -0.5
    lane = lax.broadcasted_iota(jnp.int32, (1, km_ref.shape[0]), 1)
    row = lax.broadcasted_iota(jnp.int32, (blk, blk), 0)
    col = lax.broadcasted_iota(jnp.int32, (blk, blk), 1)
    causal = col <= row
    heads = [slice(g * HEAD_DIM, (g + 1) * HEAD_DIM) for g in range(n_heads)]

    def q_block(qi, _):
        q0 = pl.multiple_of(qi * blk, blk)
        qs = [q_ref[pl.ds(q0, blk), hs] for hs in heads]
        carry = []
        for g, hs in enumerate(heads):
            bias_m = jnp.where(lane < n_meta, -cm_ref[0, g], NEG_BIG)
            sm = _dot_nt(qs[g], km_ref[:, hs]) * scale + bias_m
            m0 = jnp.max(sm, axis=1, keepdims=True)
            p0 = jnp.exp(sm - m0)
            carry.append((m0, jnp.sum(p0, axis=1, keepdims=True), _dot(p0.astype(BF16), vm_ref[:, hs])))

        def kv_block(ki, carry):
            k0 = pl.multiple_of(ki * blk, blk)
            new = []
            for g, hs in enumerate(heads):
                s = _dot_nt(qs[g], k_ref[pl.ds(k0, blk), hs]) * scale - ck_ref[0, g, pl.ds(ki, 1), :]
                new.append(_online_update(carry[g], s, v_ref[pl.ds(k0, blk), hs]))
            return tuple(new)

        carry = lax.fori_loop(0, qi, kv_block, tuple(carry))
        for g, hs in enumerate(heads):
            s = _dot_nt(qs[g], k_ref[pl.ds(q0, blk), hs]) * scale - ck_ref[0, g, pl.ds(qi, 1), :]
            s = jnp.where(causal, s, NEG_BIG)
            _, l, acc = _online_update(carry[g], s, v_ref[pl.ds(q0, blk), hs])
            o_ref[pl.ds(q0, blk), hs] = (acc / l).astype(o_ref.dtype)
        return 0

    lax.fori_loop(0, T // blk, q_block, 0)


def prompt_attention(qkv16, c_prompt, c_meta, B, T, H, meta_row, n_meta):
    blk = _pick(T, ATT_BLOCK, 128)
    G = _pick(H, ATT_HEADS, 1)
    W = G * HEAD_DIM
    HG = H // G
    ck = c_prompt.reshape(B, H, T // blk, blk)
    cm = c_meta.reshape(1, H, 1, META_PAD)
    mrow = meta_row // META_PAD
    return pl.pallas_call(
        functools.partial(_pattn_kernel, n_meta=n_meta, blk=blk, n_heads=G),
        grid=(B, HG),
        in_specs=[pl.BlockSpec((T, W), lambda b, h: (b, h)),
                  pl.BlockSpec((T, W), lambda b, h: (b, HG + h)),
                  pl.BlockSpec((T, W), lambda b, h: (b, 2 * HG + h)),
                  pl.BlockSpec((1, G, T // blk, blk), lambda b, h: (b, h, 0, 0)),
                  pl.BlockSpec((META_PAD, W), lambda b, h: (mrow, HG + h)),
                  pl.BlockSpec((META_PAD, W), lambda b, h: (mrow, 2 * HG + h)),
                  pl.BlockSpec((1, G, 1, META_PAD), lambda b, h: (0, h, 0, 0))],
        out_specs=pl.BlockSpec((T, W), lambda b, h: (b, h)),
        out_shape=jax.ShapeDtypeStruct((B * T, H * HEAD_DIM), BF16),
        compiler_params=_params("parallel", "parallel"),
        name="prompt_attention",
    )(qkv16, qkv16, qkv16, ck, qkv16, qkv16, cm)


def _mattn_kernel(q_ref, k_ref, v_ref, cm_ref, o_ref, *, n_meta):
    scale = HEAD_DIM ** -0.5
    n = q_ref.shape[0]
    row = lax.broadcasted_iota(jnp.int32, (n, n), 0)
    col = lax.broadcasted_iota(jnp.int32, (n, n), 1)
    visible = jnp.logical_and(col < n_meta, jnp.logical_or(col <= row, row >= n_meta))
    s = _dot_nt(q_ref[...], k_ref[...]) * scale - cm_ref[0, 0]
    s = jnp.where(visible, s, NEG_BIG)
    p = jnp.exp(s - jnp.max(s, axis=1, keepdims=True))
    l = jnp.sum(p, axis=1, keepdims=True)
    o_ref[...] = (_dot(p.astype(BF16), v_ref[...]) / l).astype(o_ref.dtype)


def meta_attention(qkv16, c_meta, H, meta_row, n_meta):
    mrow = meta_row // META_PAD
    cm = c_meta.reshape(1, H, 1, META_PAD)
    return pl.pallas_call(
        functools.partial(_mattn_kernel, n_meta=n_meta),
        grid=(H,),
        in_specs=[pl.BlockSpec((META_PAD, HEAD_DIM), lambda h: (mrow, h)),
                  pl.BlockSpec((META_PAD, HEAD_DIM), lambda h: (mrow, H + h)),
                  pl.BlockSpec((META_PAD, HEAD_DIM), lambda h: (mrow, 2 * H + h)),
                  pl.BlockSpec((1, 1, 1, META_PAD), lambda h: (0, h, 0, 0))],
        out_specs=pl.BlockSpec((META_PAD, HEAD_DIM), lambda h: (0, h)),
        out_shape=jax.ShapeDtypeStruct((META_PAD, H * HEAD_DIM), BF16),
        compiler_params=_params("parallel"),
        name="meta_attention",
    )(qkv16, qkv16, qkv16, cm)


def _dattn_kernel(pt_ref, q_ref, kn_ref, vn_ref, lfn_ref, *refs, n_steps, g):
    del pt_ref
    kp_refs, vp_refs, lfp_refs = refs[:g], refs[g:2 * g], refs[2 * g:3 * g]
    o_ref, m_s, l_s, acc_s, c_s = refs[3 * g:]
    step = pl.program_id(1)
    H = q_ref.shape[1]
    n_keys = kp_refs[0].shape[1]
    tiles = lfp_refs[0].shape[1]
    n = g * tiles
    scale = HEAD_DIM ** -0.5

    @pl.when(step == 0)
    def _():
        m_s[...] = jnp.full_like(m_s, NEG_BIG)
        l_s[...] = jnp.zeros_like(l_s)
        acc_s[...] = jnp.zeros_like(acc_s)
        c_s[...] = jnp.zeros_like(c_s)

    li = lax.broadcasted_iota(jnp.int32, (128, 128), 0)
    lj = lax.broadcasted_iota(jnp.int32, (128, 128), 1)
    same_head = (li % H) == (lj % H)
    w_within = jnp.where(jnp.logical_and(same_head, li // H <= lj // H), 1.0, 0.0)
    w_total = jnp.where(same_head, 1.0, 0.0)
    w = jnp.concatenate([w_within, w_total], axis=1).astype(BF16)
    lf = jnp.concatenate([r[0] for r in lfp_refs], axis=0)
    parts = _dot(jnp.concatenate(_split3(lf), axis=0), w)
    both = parts[:n] + parts[n:2 * n] + parts[2 * n:]
    within, total = both[:, :128], both[:, 128:]
    ri = lax.broadcasted_iota(jnp.int32, (n, n), 0)
    rj = lax.broadcasted_iota(jnp.int32, (n, n), 1)
    strict = jnp.where(rj < ri, 1.0, 0.0).astype(BF16)
    before = _dot(jnp.concatenate([strict] * 3, axis=1), jnp.concatenate(_split3(total), axis=0))
    carry = c_s[0:1, :]
    c_all = within + before + carry
    carry_new = before[n - 1:n] + total[n - 1:n] + carry
    c_s[...] = jnp.broadcast_to(carry_new, c_s.shape)

    q = q_ref[0]
    lane = lax.broadcasted_iota(jnp.int32, (H, 128), 1)
    sub = lax.broadcasted_iota(jnp.int32, (H, 128), 0)
    own_head = (lane % H) == sub
    pieces = []
    for i in range(g):
        s_i = _dot_nt(q, kp_refs[i][0].astype(BF16)) * scale
        for j in range(tiles):
            t = s_i[:, j * 128:(j + 1) * 128] - c_all[i * tiles + j:i * tiles + j + 1, :]
            pieces.append(jnp.where(own_head, t, NEG_BIG))
    s = jnp.concatenate(pieces, axis=1)

    m, l = m_s[:, 0:1], l_s[:, 0:1]
    m_new = jnp.maximum(m, jnp.max(s, axis=1, keepdims=True))
    alpha = jnp.exp(m - m_new)
    p = jnp.exp(s - m_new)
    l_new = alpha * l + jnp.sum(p, axis=1, keepdims=True)
    pv = _dot(p[:, :n_keys].astype(BF16), vp_refs[0][0].astype(BF16))
    for i in range(1, g):
        pv = pv + _dot(p[:, i * n_keys:(i + 1) * n_keys].astype(BF16), vp_refs[i][0].astype(BF16))
    acc = alpha * acc_s[...] + pv
    m_s[...] = jnp.broadcast_to(m_new, m_s.shape)
    l_s[...] = jnp.broadcast_to(l_new, l_s.shape)
    acc_s[...] = acc

    @pl.when(step == n_steps - 1)
    def _():
        c_past = jnp.sum(jnp.where(lane == sub, jnp.broadcast_to(carry_new, (H, 128)), 0.0), axis=1, keepdims=True)
        c_new = c_past + lfn_ref[0]
        s_new = jnp.sum(q.astype(F32) * kn_ref[0].astype(F32), axis=1, keepdims=True) * scale - c_new
        m2 = jnp.maximum(m_new, s_new)
        a2 = jnp.exp(m_new - m2)
        p_new = jnp.exp(s_new - m2)
        l2 = a2 * l_new + p_new
        o_ref[0] = ((a2 * acc + p_new * vn_ref[0].astype(F32)) / l2).astype(o_ref.dtype)


def decode_attention(page_table, q, k_new, v_new, lf_new, cache_k, cache_v, cache_lf):
    DB, H, _ = q.shape
    n_pages = page_table.shape[1]
    n_keys = cache_k.shape[1]
    tiles = cache_lf.shape[1]
    assert 128 % H == 0 and tiles * 128 == n_keys
    g = _pick(n_pages, DEC_PAGES, 1)
    n_steps = n_pages // g

    def page_map(i):
        return lambda b, p, pt: (pt[b * n_pages + p * g + i], 0, 0)

    per_seq = lambda b, p, pt: (b, 0, 0)
    grid_spec = pltpu.PrefetchScalarGridSpec(
        num_scalar_prefetch=1,
        grid=(DB, n_steps),
        in_specs=([pl.BlockSpec((1, H, HEAD_DIM), per_seq)] * 3 + [pl.BlockSpec((1, H, 1), per_seq)]
                  + [pl.BlockSpec((1, n_keys, HEAD_DIM), page_map(i)) for i in range(g)]
                  + [pl.BlockSpec((1, n_keys, HEAD_DIM), page_map(i)) for i in range(g)]
                  + [pl.BlockSpec((1, tiles, 128), page_map(i)) for i in range(g)]),
        out_specs=pl.BlockSpec((1, H, HEAD_DIM), per_seq),
        scratch_shapes=[pltpu.VMEM((H, 128), F32), pltpu.VMEM((H, 128), F32),
                        pltpu.VMEM((H, HEAD_DIM), F32), pltpu.VMEM((8, 128), F32)],
    )
    return pl.pallas_call(
        functools.partial(_dattn_kernel, n_steps=n_steps, g=g),
        grid_spec=grid_spec,
        out_shape=jax.ShapeDtypeStruct((DB, H, HEAD_DIM), BF16),
        compiler_params=_params("parallel", "arbitrary"),
        name="decode_attention",
    )(page_table.reshape(-1), q, k_new, v_new, lf_new, *([cache_k] * g), *([cache_v] * g), *([cache_lf] * g))


def _lru_gates(xc, wa_ref, ba, wx_ref, bx, lam):
    xcb = xc.astype(BF16)
    nb = xc.shape[1] // LRU_BLOCK
    ra = jnp.concatenate([_dot(xcb[:, n * LRU_BLOCK:(n + 1) * LRU_BLOCK], wa_ref[n]) for n in range(nb)], axis=1)
    rx = jnp.concatenate([_dot(xcb[:, n * LRU_BLOCK:(n + 1) * LRU_BLOCK], wx_ref[n]) for n in range(nb)], axis=1)
    r = jax.nn.sigmoid(ra + ba)
    i = jax.nn.sigmoid(rx + bx)
    log_a = (-LRU_C) * r * _softplus(-lam)
    a = jnp.exp(log_a)
    u = jnp.sqrt(-jnp.tanh(log_a) * (a * a + 1.0)) * (i * xc)
    return a, u


def _rnn_kernel(xb_ref, yb_ref, h0_ref, c0_ref, cw_ref, cb_ref, wa_ref, ba_ref, wx_ref, bx_ref, lam_ref,
                y_ref, hl_ref, xprev_s, h_s, a_s, u_s):
    @pl.when(pl.program_id(1) == 0)
    def _():
        xprev_s[...] = c0_ref[0]
        h_s[...] = h0_ref[0]

    xb = xb_ref[...]
    n = xb.shape[0]
    xp = jnp.concatenate([xprev_s[...], xb], axis=0)
    xc = cb_ref[...] + pltpu.roll(xp, 3, 0)[8:] * cw_ref[0:1, :]
    xc = xc + pltpu.roll(xp, 2, 0)[8:] * cw_ref[1:2, :]
    xc = xc + pltpu.roll(xp, 1, 0)[8:] * cw_ref[2:3, :]
    xc = xc + xb * cw_ref[3:4, :]
    a, u = _lru_gates(xc, wa_ref, ba_ref[...], wx_ref, bx_ref[...], lam_ref[...])
    a_s[...] = a
    u_s[...] = u

    def step(t, h):
        h = a_s[pl.ds(t, 1), :] * h + u_s[pl.ds(t, 1), :]
        u_s[pl.ds(t, 1), :] = h
        return h

    h = lax.fori_loop(0, n, step, h_s[0:1, :], unroll=8)
    h_s[...] = jnp.broadcast_to(h, h_s.shape)
    hl_ref[0] = h_s[...]
    xprev_s[...] = xb[n - 8:, :]
    y_ref[...] = (u_s[...] * jax.nn.gelu(yb_ref[...])).astype(y_ref.dtype)


def rnn_sequences(gates, xb_col, yb_col, row_start, n_seq, seq_len, chunk, h0, c0, lru):
    cw, cb, wa, ba, wx, bx, lam = lru
    C = cw.shape[1]
    nb = C // LRU_BLOCK
    nc = seq_len // chunk
    assert row_start % chunk == 0 and xb_col % C == 0 and yb_col % C == 0
    r0, jx, jy = row_start // chunk, xb_col // C, yb_col // C
    vec = pl.BlockSpec((1, C), lambda s, c: (0, 0))
    return pl.pallas_call(
        _rnn_kernel,
        grid=(n_seq, nc),
        in_specs=[pl.BlockSpec((chunk, C), lambda s, c: (r0 + s * nc + c, jx)),
                  pl.BlockSpec((chunk, C), lambda s, c: (r0 + s * nc + c, jy)),
                  pl.BlockSpec((1, 8, C), lambda s, c: (s, 0, 0)),
                  pl.BlockSpec((1, 8, C), lambda s, c: (s, 0, 0)),
                  pl.BlockSpec((cw.shape[0], C), lambda s, c: (0, 0)),
                  vec,
                  pl.BlockSpec((nb, LRU_BLOCK, LRU_BLOCK), lambda s, c: (0, 0, 0)),
                  vec,
                  pl.BlockSpec((nb, LRU_BLOCK, LRU_BLOCK), lambda s, c: (0, 0, 0)),
                  vec, vec],
        out_specs=[pl.BlockSpec((chunk, C), lambda s, c: (s * nc + c, 0)),
                   pl.BlockSpec((1, 8, C), lambda s, c: (s, 0, 0))],
        out_shape=[jax.ShapeDtypeStruct((n_seq * seq_len, C), BF16),
                   jax.ShapeDtypeStruct((n_seq, 8, C), F32)],
        scratch_shapes=[pltpu.VMEM((8, C), F32), pltpu.VMEM((8, C), F32),
                        pltpu.VMEM((chunk, C), F32), pltpu.VMEM((chunk, C), F32)],
        compiler_params=_params("parallel", "arbitrary"),
        name="rnn_sequences",
    )(gates, gates, h0, c0, cw, cb, wa, ba, wx, bx, lam)


def _rnn_step_kernel(xb_ref, yb_ref, h0_ref, s0_ref, s1_ref, s2_ref, cw_ref, cb_ref, wa_ref, ba_ref,
                     wx_ref, bx_ref, lam_ref, y_ref, h_ref):
    xb = xb_ref[...]
    xc = cb_ref[...] + s0_ref[...] * cw_ref[0:1, :]
    xc = xc + s1_ref[...] * cw_ref[1:2, :]
    xc = xc + s2_ref[...] * cw_ref[2:3, :]
    xc = xc + xb * cw_ref[3:4, :]
    a, u = _lru_gates(xc, wa_ref, ba_ref[...], wx_ref, bx_ref[...], lam_ref[...])
    h = a * h0_ref[...] + u
    h_ref[...] = h
    y_ref[...] = (h * jax.nn.gelu(yb_ref[...])).astype(y_ref.dtype)


def rnn_step(gates, xb_col, yb_col, row_start, n_rows, h0, s0, s1, s2, lru):
    cw, cb, wa, ba, wx, bx, lam = lru
    C = cw.shape[1]
    nb = C // LRU_BLOCK
    assert row_start % n_rows == 0
    r0, jx, jy = row_start // n_rows, xb_col // C, yb_col // C
    full = pl.BlockSpec((n_rows, C), lambda i: (0, 0))
    vec = pl.BlockSpec((1, C), lambda i: (0, 0))
    return pl.pallas_call(
        _rnn_step_kernel,
        grid=(1,),
        in_specs=[pl.BlockSpec((n_rows, C), lambda i: (r0, jx)),
                  pl.BlockSpec((n_rows, C), lambda i: (r0, jy)),
                  full, full, full, full,
                  pl.BlockSpec((cw.shape[0], C), lambda i: (0, 0)),
                  vec,
                  pl.BlockSpec((nb, LRU_BLOCK, LRU_BLOCK), lambda i: (0, 0, 0)),
                  vec,
                  pl.BlockSpec((nb, LRU_BLOCK, LRU_BLOCK), lambda i: (0, 0, 0)),
                  vec, vec],
        out_specs=[full, full],
        out_shape=[jax.ShapeDtypeStruct((n_rows, C), BF16), jax.ShapeDtypeStruct((n_rows, C), F32)],
        compiler_params=_params("arbitrary"),
        name="rnn_step",
    )(gates, gates, h0, s0, s1, s2, cw, cb, wa, ba, wx, bx, lam)


def _half_ffn(x, g, w1, w3, w2, tm):
    D, FF = w1.shape
    ffp = -(-FF // 1024) * 1024
    w1p = cast_pad(w1, 1, ffp)
    w3p = cast_pad(w3, 1, ffp)
    w2p = cast_pad(w2, 0, ffp)
    h = rmsnorm(x, g, BF16)
    act = ffn_up(h, w1p, w3p, tm)
    return mm_residual(act, w2p, x, 0.5, tm, _pick(D, 1024, 128), _pick(ffp, TK_TARGET, 256))


def kernel(x_prompt, x_sample, cache_k, cache_v, cache_logf, state_h, state_conv, page_table, meta,
           norm_ffn1, ffn1_w1, ffn1_w3, ffn1_w2, norm_mix, w_in, b_f, b_gate, conv_w, conv_b,
           lru_wa, lru_ba, lru_wx, lru_bx, lru_lambda, w_pa, w_pb, w_o,
           norm_ffn2, ffn2_w1, ffn2_w3, ffn2_w2, norm_final):
    B, T, D = x_prompt.shape
    DB, dec_seq, _ = x_sample.shape
    depth, n_pool, page, H, hd = cache_k.shape
    n_meta = meta.shape[0]
    C = state_h.shape[-1]
    conv_width = conv_w.shape[1]
    DA = H * hd
    assert depth == 1 and dec_seq == 1 and hd == HEAD_DIM and conv_width == 4
    assert n_meta <= META_PAD and DB % 128 == 0 and page == 128
    assert w_in.shape[-1] == 3 * DA + H + 2 * C + 2 * D

    Rm = B * T
    s_row, m_row = Rm, Rm + DB
    R = Rm + DB + META_PAD
    tm = _pick(R, TM_TARGET, 16)

    x0 = jnp.concatenate([x_prompt.reshape(Rm, D), x_sample.reshape(DB, D), meta.astype(F32),
                          jnp.zeros((META_PAD - n_meta, D), F32)], axis=0)

    x1 = _half_ffn(x0, norm_ffn1[0], ffn1_w1[0], ffn1_w3[0], ffn1_w2[0], tm)

    h2 = rmsnorm(x1, norm_mix[0], BF16)
    w = w_in[0]
    o_f = 3 * DA
    o_x = o_f + H
    w_qkv = w[:, :o_f].astype(BF16)
    f_pad = TN - H
    w_rest = regroup_in_proj(w, o_f, H, 2 * C + 2 * D)
    bias = jnp.concatenate([jnp.zeros((2 * C,), F32), b_gate[0], b_f[0], jnp.zeros((f_pad,), F32)]).reshape(1, -1)
    qkv32, qkv16 = mm_dual(h2, w_qkv, tm)
    gates = mm_gate(h2, w_rest, bias, tm, 2 * C, 2 * D)
    col_ga, col_gb, col_f = 2 * C, 2 * C + D, 2 * C + 2 * D

    c_prompt = log_forget_cumsum(gates, col_f, 0, B, T, H, tb=_pick(T, 256, 128))
    c_meta = log_forget_cumsum(gates, col_f, m_row, 1, META_PAD, H, tb=META_PAD, n_valid=n_meta,
                               subtract_total=True)
    y_att_p = prompt_attention(qkv16, c_prompt, c_meta, B, T, H, m_row, n_meta)
    y_att_m = meta_attention(qkv16, c_meta, H, m_row, n_meta)
    samp16 = qkv16[s_row:s_row + DB]
    heads = lambda t: t.reshape(DB, H, HEAD_DIM)
    lf_s = gates[s_row:s_row + DB, col_f:col_f + H]
    y_att_s = decode_attention(
        page_table, heads(samp16[:, :DA]), heads(samp16[:, DA:2 * DA]), heads(samp16[:, 2 * DA:]),
        lf_s.reshape(DB, H, 1), cache_k.reshape(n_pool, page * H, HEAD_DIM),
        cache_v.reshape(n_pool, page * H, HEAD_DIM), cache_logf.reshape(n_pool, page * H // 128, 128))
    y_att = jnp.concatenate([y_att_p, y_att_s.reshape(DB, DA), y_att_m], axis=0)

    lru = (conv_w[0], conv_b[0].reshape(1, C), lru_wa[0].astype(BF16), lru_ba[0].reshape(1, C),
           lru_wx[0].astype(BF16), lru_bx[0].reshape(1, C), lru_lambda[0].reshape(1, C))
    n_meta8 = -(-n_meta // 8) * 8
    assert n_meta8 == n_meta and n_meta >= 8
    zeros8 = jnp.zeros((1, 8, C), F32)
    y_rnn_m, h_meta = rnn_sequences(gates, 0, C, m_row, 1, n_meta, n_meta, zeros8, zeros8, lru)
    xb_meta_tail = gates[m_row + n_meta - 8:m_row + n_meta, :C]
    y_rnn_p, h_prompt = rnn_sequences(
        gates, 0, C, 0, B, T, _pick(T, RNN_CHUNK, 8),
        jnp.broadcast_to(h_meta, (B, 8, C)), jnp.broadcast_to(xb_meta_tail[None], (B, 8, C)), lru)
    sc = state_conv[0]
    y_rnn_s, h_sample = rnn_step(gates, 0, C, s_row, DB, state_h[0], sc[:, 0], sc[:, 1], sc[:, 2], lru)
    y_rnn = jnp.concatenate([y_rnn_p, y_rnn_s, y_rnn_m, jnp.zeros((META_PAD - n_meta, C), BF16)], axis=0)

    merged = merge_proj(y_att, y_rnn, w_pa[0].astype(BF16), w_pb[0].astype(BF16), gates, col_ga, col_gb, tm)
    x2 = mm_residual(merged, w_o[0].astype(BF16), x1, 1.0, tm, TN, D)
    x3 = _half_ffn(x2, norm_ffn2[0], ffn2_w1[0], ffn2_w3[0], ffn2_w2[0], tm)

    y_prompt = rmsnorm(x3, norm_final, F32, row_start=0, n_rows=Rm).reshape(B, T, D)
    y_sample = rmsnorm(x3, norm_final, F32, row_start=s_row, n_rows=DB).reshape(DB, 1, D)

    def with_meta(full, lo, hi, tail):
        real = full[:Rm, lo:hi].reshape((B, T) + tail)
        m = jnp.broadcast_to(full[m_row:m_row + n_meta, lo:hi].reshape((1, n_meta) + tail), (B, n_meta) + tail)
        return jnp.concatenate([m, real], axis=1)[None]

    new_k_p = with_meta(qkv32, DA, 2 * DA, (H, HEAD_DIM))
    new_v_p = with_meta(qkv32, 2 * DA, 3 * DA, (H, HEAD_DIM))
    new_lf_p = with_meta(gates, col_f, col_f + H, (H,))
    new_h_p = h_prompt[:, 0][None]
    xb_p = gates[:Rm, :C].reshape(B, T, C)
    new_conv_p = xb_p[:, T - (conv_width - 1):][None]
    samp32 = qkv32[s_row:s_row + DB]
    new_k_s = samp32[:, DA:2 * DA].reshape(1, DB, 1, H, HEAD_DIM)
    new_v_s = samp32[:, 2 * DA:].reshape(1, DB, 1, H, HEAD_DIM)
    new_lf_s = lf_s.reshape(1, DB, 1, H)
    new_h_s = h_sample[None]
    new_conv_s = jnp.concatenate([sc[:, 1:], gates[s_row:s_row + DB, :C][:, None]], axis=1)[None]
    return (y_prompt, y_sample, new_k_p, new_v_p, new_lf_p, new_h_p, new_conv_p,
            new_k_s, new_v_s, new_lf_s, new_h_s, new_conv_s)
```
